```python
import math
import jax, jax.numpy as jnp
from jax import lax
import numpy as np

D_MODEL = 1024
BATCH = 1
SEQ = 16384
DEPTH = 1
DEC_BATCH = 2
DEC_SEQ = 8192
PAST_LEN = 128

CONV_WIDTH = 512
CONV_KERNEL = 31
N_HEADS = 4
HEAD_DIM = 64
V_DIM = 2 * HEAD_DIM
ATTN_WIDTH = N_HEADS * V_DIM
QK_WIDTH = N_HEADS * 2 * HEAD_DIM
ROPE_THETA = 10000.0
Q_BLOCK = 128
ALPHA = (2.0 * DEPTH) ** 0.25
BETA = (8.0 * DEPTH) ** -0.25
LN_EPS = 1e-5

SPLIT_SIZES = (CONV_WIDTH, CONV_WIDTH, CONV_WIDTH,
               QK_WIDTH, QK_WIDTH, ATTN_WIDTH,
               ATTN_WIDTH,
               D_MODEL, D_MODEL)
IN_WIDTH = sum(SPLIT_SIZES)
SPLIT_POINTS = tuple(int(v) for v in np.cumsum(SPLIT_SIZES)[:-1])

kernel_name = "hybrid_conformer_diffattn_encoder"


def layer_norm(x, g, b):
    xf = x.astype(jnp.float32)
    mu = jnp.mean(xf, axis=-1, keepdims=True)
    var = jnp.mean(jnp.square(xf - mu), axis=-1, keepdims=True)
    y = (xf - mu) * lax.rsqrt(var + LN_EPS) * g.astype(jnp.float32) + b.astype(jnp.float32)
    return y.astype(x.dtype)


def rms_norm(x, g):
    xf = x.astype(jnp.float32)
    y = xf * lax.rsqrt(jnp.mean(jnp.square(xf), axis=-1, keepdims=True) + LN_EPS) * g.astype(jnp.float32)
    return y.astype(x.dtype)


def apply_rope(t):
    S = t.shape[1]
    pos = jnp.arange(S, dtype=jnp.float32)
    inv_freq = 1.0 / jnp.power(ROPE_THETA, jnp.arange(0, HEAD_DIM, 2, dtype=jnp.float32) / HEAD_DIM)
    ang = pos[:, None] * inv_freq[None, :]
    cos = jnp.concatenate([jnp.cos(ang), jnp.cos(ang)], axis=-1)[None, :, None, None, :]
    sin = jnp.concatenate([jnp.sin(ang), jnp.sin(ang)], axis=-1)[None, :, None, None, :]
    tf = t.astype(jnp.float32)
    t1, t2 = jnp.split(tf, 2, axis=-1)
    rot = jnp.concatenate([-t2, t1], axis=-1)
    return (tf * cos + rot * sin).astype(t.dtype)


def diff_attention(q, k, v, lam):
    B, S = q.shape[0], q.shape[1]
    n_blk = S // Q_BLOCK
    qb = q.reshape(B, n_blk, Q_BLOCK, N_HEADS, 2, HEAD_DIM).transpose(1, 0, 2, 3, 4, 5)
    scale = 1.0 / math.sqrt(HEAD_DIM)

    def block(q_blk):
        s = jnp.einsum('bqhcd,bkhcd->bhcqk', q_blk, k).astype(jnp.float32) * scale
        p = jax.nn.softmax(s, axis=-1)
        a = p[:, :, 0] - lam * p[:, :, 1]
        return jnp.einsum('bhqk,bkhe->bqhe', a.astype(v.dtype), v)

    out = lax.map(block, qb)
    return out.transpose(1, 0, 2, 3, 4).reshape(B, S, N_HEADS, V_DIM)


def encoder_layer(x, w_in, b_in, conv_dw, conv_dw_b, conv_ln_g, conv_ln_b, w_conv_proj, b_conv_proj,
                  lam_q1, lam_k1, lam_q2, lam_k2, subln_g, w_attn_o, w_out, b_out, ln_g, ln_b, lam_init):
    B, S, _ = x.shape
    h = x @ w_in + b_in
    glu_a, glu_b, conv_gate, q, k, v, attn_gate, m_conv, m_attn = jnp.split(h, SPLIT_POINTS, axis=-1)

    u = glu_a * jax.nn.sigmoid(glu_b)
    u = lax.conv_general_dilated(u, conv_dw[:, None, :].astype(u.dtype), window_strides=(1,),
                                 padding=[((CONV_KERNEL - 1) // 2, (CONV_KERNEL - 1) // 2)],
                                 dimension_numbers=('NWC', 'WIO', 'NWC'),
                                 feature_group_count=CONV_WIDTH) + conv_dw_b
    u = jax.nn.silu(layer_norm(u, conv_ln_g, conv_ln_b)) * jax.nn.silu(conv_gate)
    y_conv = u @ w_conv_proj + b_conv_proj

    q = apply_rope(q.reshape(B, S, N_HEADS, 2, HEAD_DIM))
    k = apply_rope(k.reshape(B, S, N_HEADS, 2, HEAD_DIM))
    v = v.reshape(B, S, N_HEADS, V_DIM)
    lam = (jnp.exp(jnp.sum(lam_q1.astype(jnp.float32) * lam_k1.astype(jnp.float32)))
           - jnp.exp(jnp.sum(lam_q2.astype(jnp.float32) * lam_k2.astype(jnp.float32))) + lam_init)
    o = diff_attention(q, k, v, lam)
    o = rms_norm(o, subln_g) * (1.0 - lam_init)
    o = o.reshape(B, S, ATTN_WIDTH) * jax.nn.silu(attn_gate)
    y_attn = o @ w_attn_o

    merged = jax.nn.sigmoid(m_conv) * y_conv + jax.nn.sigmoid(m_attn) * y_attn
    out = merged @ w_out + b_out
    return layer_norm(ALPHA * x + out, ln_g, ln_b)


def setup_inputs(seed: int = 0) -> dict:
    key = jax.random.key(seed)
    ks = jax.random.split(key, 24)
    f32 = jnp.float32
    nrm = lambda k, shape, s: jax.random.normal(k, shape, f32) * s
    x_prompt = nrm(ks[0], (BATCH, SEQ, D_MODEL), 1.0)
    x_sample = nrm(ks[1], (DEC_BATCH, DEC_SEQ, D_MODEL), 1.0)
    w_in = nrm(ks[2], (DEPTH, D_MODEL, IN_WIDTH), D_MODEL ** -0.5)
    v_lo = SPLIT_POINTS[4]
    v_hi = SPLIT_POINTS[5]
    col_scale = jnp.ones((IN_WIDTH,), f32).at[v_lo:v_hi].set(BETA)
    w_in = w_in * col_scale
    b_in = nrm(ks[3], (DEPTH, IN_WIDTH), 0.02)
    conv_dw = nrm(ks[4], (DEPTH, CONV_KERNEL, CONV_WIDTH), CONV_KERNEL ** -0.5)
    conv_dw_b = nrm(ks[5], (DEPTH, CONV_WIDTH), 0.02)
    conv_ln_g = 1.0 + nrm(ks[6], (DEPTH, CONV_WIDTH), 0.02)
    conv_ln_b = nrm(ks[7], (DEPTH, CONV_WIDTH), 0.02)
    w_conv_proj = nrm(ks[8], (DEPTH, CONV_WIDTH, D_MODEL), CONV_WIDTH ** -0.5)
    b_conv_proj = nrm(ks[9], (DEPTH, D_MODEL), 0.02)
    lam_q1 = nrm(ks[10], (DEPTH, HEAD_DIM), 0.1)
    lam_k1 = nrm(ks[11], (DEPTH, HEAD_DIM), 0.1)
    lam_q2 = nrm(ks[12], (DEPTH, HEAD_DIM), 0.1)
    lam_k2 = nrm(ks[13], (DEPTH, HEAD_DIM), 0.1)
    subln_g = 1.0 + nrm(ks[14], (DEPTH, V_DIM), 0.02)
    w_attn_o = nrm(ks[15], (DEPTH, ATTN_WIDTH, D_MODEL), ATTN_WIDTH ** -0.5)
    w_out = nrm(ks[16], (DEPTH, D_MODEL, D_MODEL), BETA * D_MODEL ** -0.5)
    b_out = nrm(ks[17], (DEPTH, D_MODEL), 0.02)
    ln_g = 1.0 + nrm(ks[18], (DEPTH, D_MODEL), 0.02)
    ln_b = nrm(ks[19], (DEPTH, D_MODEL), 0.02)
    return {"x_prompt": x_prompt, "x_sample": x_sample, "w_in": w_in, "b_in": b_in,
            "conv_dw": conv_dw, "conv_dw_b": conv_dw_b, "conv_ln_g": conv_ln_g, "conv_ln_b": conv_ln_b,
            "w_conv_proj": w_conv_proj, "b_conv_proj": b_conv_proj,
            "lam_q1": lam_q1, "lam_k1": lam_k1, "lam_q2": lam_q2, "lam_k2": lam_k2,
            "subln_g": subln_g, "w_attn_o": w_attn_o, "w_out": w_out, "b_out": b_out,
            "ln_g": ln_g, "ln_b": ln_b}


def reference(x_prompt, x_sample, w_in, b_in, conv_dw, conv_dw_b, conv_ln_g, conv_ln_b, w_conv_proj,
              b_conv_proj, lam_q1, lam_k1, lam_q2, lam_k2, subln_g, w_attn_o, w_out, b_out, ln_g, ln_b):
    y_prompt = x_prompt
    y_sample = x_sample
    for l in range(DEPTH):
        lam_init = 0.8 - 0.6 * math.exp(-0.3 * l)
        y_prompt = encoder_layer(y_prompt, w_in[l], b_in[l], conv_dw[l], conv_dw_b[l], conv_ln_g[l],
                                 conv_ln_b[l], w_conv_proj[l], b_conv_proj[l], lam_q1[l], lam_k1[l],
                                 lam_q2[l], lam_k2[l], subln_g[l], w_attn_o[l], w_out[l], b_out[l],
                                 ln_g[l], ln_b[l], lam_init)
        y_sample = encoder_layer(y_sample, w_in[l], b_in[l], conv_dw[l], conv_dw_b[l], conv_ln_g[l],
                                 conv_ln_b[l], w_conv_proj[l], b_conv_proj[l], lam_q1[l], lam_k1[l],
                                 lam_q2[l], lam_k2[l], subln_g[l], w_attn_o[l], w_out[l], b_out[l],
                                 ln_g[l], ln_b[l], lam_init)
    return (y_prompt, y_sample)
```

```python
import functools
import math

import jax
import jax.numpy as jnp
from jax import lax
from jax.experimental import pallas as pl
from jax.experimental.pallas import tpu as pltpu

D_MODEL = 1024
CONV_WIDTH = 512
CONV_KERNEL = 31
CONV_PAD = (CONV_KERNEL - 1) // 2
N_HEADS = 4
HEAD_DIM = 64
V_DIM = 2 * HEAD_DIM
ATTN_WIDTH = N_HEADS * V_DIM
QK_WIDTH = N_HEADS * 2 * HEAD_DIM
ROPE_THETA = 10000.0
LN_EPS = 1e-5

LANES = 128
SUBLANES = 8
HALO_ROWS = 16
VMEM_LIMIT_BYTES = 56 * 1024 * 1024

PROJ_ROWS = 512
OUT_ROWS = 256
CONV_ROW_CHUNK = 32
ATTN_Q_ROWS = 256
ATTN_K_ROWS = 512

_BF16 = jnp.bfloat16
_F32 = jnp.float32


def _dot(a, b):
    return jnp.dot(a, b, preferred_element_type=_F32)


def _rope(t, cos, sin_lo, sin_hi):
    up = pltpu.roll(t, LANES - HEAD_DIM // 2, 1)
    down = pltpu.roll(t, HEAD_DIM // 2, 1)
    return t * cos + up * sin_lo + down * sin_hi


def _proj_kernel(x_ref, w_ref, b_ref, cos_ref, sinlo_ref, sinhi_ref, u_ref, q_ref, k_ref, v_ref):
    xb = x_ref[...].astype(_BF16)
    W = CONV_WIDTH

    def cols(j):
        return _dot(xb, w_ref[:, j * W:(j + 1) * W]) + b_ref[:, j * W:(j + 1) * W]

    u_ref[...] = cols(0) * jax.nn.sigmoid(cols(1))
    cos, sin_lo, sin_hi = cos_ref[...], sinlo_ref[...], sinhi_ref[...]
    scale = 1.0 / math.sqrt(HEAD_DIM)
    tq = cols(2)
    tk = cols(3)
    for h in range(N_HEADS):
        sl = slice(h * V_DIM, (h + 1) * V_DIM)
        q_ref[:, sl] = (_rope(tq[:, sl], cos, sin_lo, sin_hi) * scale).astype(_BF16)
        k_ref[:, sl] = _rope(tk[:, sl], cos, sin_lo, sin_hi).astype(_BF16)
    v_ref[...] = cols(4).astype(_BF16)


def _proj_call(x, w_a, b_a, cos, sin_lo, sin_hi, seq_len):
    n_tok = x.shape[0]
    tm = PROJ_ROWS
    n_pos_blocks = seq_len // tm
    const = lambda i: (0, 0)
    row = lambda i: (i, 0)
    pos = lambda i: (i % n_pos_blocks, 0)
    wide = w_a.shape[1]
    return pl.pallas_call(
        _proj_kernel,
        grid=(n_tok // tm,),
        in_specs=[
            pl.BlockSpec((tm, D_MODEL), row),
            pl.BlockSpec((D_MODEL, wide), const, pipeline_mode=pl.Buffered(1)),
            pl.BlockSpec((1, wide), const),
            pl.BlockSpec((tm, LANES), pos),
            pl.BlockSpec((tm, LANES), pos),
            pl.BlockSpec((tm, LANES), pos),
        ],
        out_specs=[
            pl.BlockSpec((tm, CONV_WIDTH), row),
            pl.BlockSpec((tm, QK_WIDTH), row),
            pl.BlockSpec((tm, QK_WIDTH), row),
            pl.BlockSpec((tm, ATTN_WIDTH), row),
        ],
        out_shape=[
            jax.ShapeDtypeStruct((n_tok, CONV_WIDTH), _F32),
            jax.ShapeDtypeStruct((n_tok, QK_WIDTH), _BF16),
            jax.ShapeDtypeStruct((n_tok, QK_WIDTH), _BF16),
            jax.ShapeDtypeStruct((n_tok, ATTN_WIDTH), _BF16),
        ],
        compiler_params=pltpu.CompilerParams(
            dimension_semantics=("arbitrary",), vmem_limit_bytes=VMEM_LIMIT_BYTES),
        name="proj",
    )(x, w_a, b_a, cos, sin_lo, sin_hi)


def _attn_kernel(lq1_ref, lk1_ref, lq2_ref, lk2_ref, g_ref, q_ref, k_ref, v_ref, o_ref,
                 m_sc, l_sc, acc_sc, *, seq_len, lam_init):
    tq = q_ref.shape[0]
    tk = ATTN_K_ROWS
    q = q_ref[...]
    lane = lax.broadcasted_iota(jnp.int32, q.shape, 1)
    zero = jnp.zeros_like(q)
    qz = jnp.concatenate([jnp.where(lane < HEAD_DIM, q, zero), jnp.where(lane >= HEAD_DIM, q, zero)], axis=0)

    m_sc[...] = jnp.full(m_sc.shape, -jnp.inf, _F32)
    l_sc[...] = jnp.zeros(l_sc.shape, _F32)
    acc_sc[...] = jnp.zeros(acc_sc.shape, _F32)

    def body(j, carry):
        start = pl.multiple_of(j * tk, tk)
        kc = k_ref[pl.ds(start, tk), :]
        vc = v_ref[pl.ds(start, tk), :]
        s = lax.dot_general(qz, kc, (((1,), (1,)), ((), ())), preferred_element_type=_F32)
        m_prev = m_sc[...]
        m_new = jnp.maximum(m_prev, jnp.max(s, axis=1, keepdims=True))
        alpha = jnp.exp(m_prev - m_new)
        p = jnp.exp(s - pltpu.repeat(m_new, tk // LANES, axis=1))
        l_sc[...] = alpha * l_sc[...] + jnp.sum(p, axis=1, keepdims=True)
        acc_sc[...] = alpha * acc_sc[...] + _dot(p.astype(_BF16), vc)
        m_sc[...] = m_new
        return carry

    lax.fori_loop(0, seq_len // tk, body, 0)

    o_all = acc_sc[...] / l_sc[...]
    lam = (jnp.exp(jnp.sum(lq1_ref[...] * lk1_ref[...], axis=1, keepdims=True))
           - jnp.exp(jnp.sum(lq2_ref[...] * lk2_ref[...], axis=1, keepdims=True)) + lam_init)
    o = o_all[:tq] - lam * o_all[tq:]
    o = o * lax.rsqrt(jnp.mean(jnp.square(o), axis=-1, keepdims=True) + LN_EPS) * g_ref[...]
    o_ref[...] = o * (1.0 - lam_init)


def _attn_call(q, k, v, lq1, lk1, lq2, lk2, subln_g, seq_len, lam_init):
    n_tok = q.shape[0]
    n_seq = n_tok // seq_len
    tq = ATTN_Q_ROWS
    n_qb = seq_len // tq
    small = lambda b, h, i: (0, 0)
    qmap = lambda b, h, i: (b * n_qb + i, h)
    kvmap = lambda b, h, i: (b, h)
    kernel = functools.partial(_attn_kernel, seq_len=seq_len, lam_init=lam_init)
    return pl.pallas_call(
        kernel,
        grid=(n_seq, N_HEADS, n_qb),
        in_specs=[
            pl.BlockSpec((1, HEAD_DIM), small),
            pl.BlockSpec((1, HEAD_DIM), small),
            pl.BlockSpec((1, HEAD_DIM), small),
            pl.BlockSpec((1, HEAD_DIM), small),
            pl.BlockSpec((1, V_DIM), small),
            pl.BlockSpec((tq, V_DIM), qmap),
            pl.BlockSpec((seq_len, V_DIM), kvmap),
            pl.BlockSpec((seq_len, V_DIM), kvmap),
        ],
        out_specs=pl.BlockSpec((tq, V_DIM), qmap),
        out_shape=jax.ShapeDtypeStruct((n_tok, ATTN_WIDTH), _F32),
        scratch_shapes=[
            pltpu.VMEM((2 * tq, LANES), _F32),
            pltpu.VMEM((2 * tq, LANES), _F32),
            pltpu.VMEM((2 * tq, V_DIM), _F32),
        ],
        compiler_params=pltpu.CompilerParams(
            dimension_semantics=("arbitrary", "arbitrary", "arbitrary"), vmem_limit_bytes=VMEM_LIMIT_BYTES),
        name="attn",
    )(lq1, lk1, lq2, lk2, subln_g, q, k, v)


def _layer_norm(z, g, b):
    mu = jnp.mean(z, axis=-1, keepdims=True)
    zc = z - mu
    var = jnp.mean(jnp.square(zc), axis=-1, keepdims=True)
    return zc * lax.rsqrt(var + LN_EPS) * g + b


def _out_kernel(x_ref, up_ref, uc_ref, un_ref, o_ref, wg_ref, bg_ref, dw_ref, dwb_ref, cg_ref, cb_ref,
                wc_ref, bc_ref, wa_ref, wo_ref, bo_ref, g_ref, b_ref, y_ref, ubuf, conv_sc,
                *, tiles_per_seq, alpha):
    tm = x_ref.shape[0]
    i = pl.program_id(0)
    pos = i % tiles_per_seq
    ubuf[0:HALO_ROWS, :] = jnp.where(pos > 0, up_ref[...], 0.0)
    ubuf[HALO_ROWS:HALO_ROWS + tm, :] = uc_ref[...]
    ubuf[HALO_ROWS + tm:2 * HALO_ROWS + tm, :] = jnp.where(pos < tiles_per_seq - 1, un_ref[...], 0.0)

    rc = CONV_ROW_CHUNK
    base = HALO_ROWS - CONV_PAD
    for r in range(0, tm, rc):
        acc = jnp.broadcast_to(dwb_ref[...], (rc, CONV_WIDTH))
        for t in range(CONV_KERNEL):
            acc = acc + ubuf[base + r + t:base + r + t + rc, :] * dw_ref[t:t + 1, :]
        conv_sc[r:r + rc, :] = acc

    xb = x_ref[...].astype(_BF16)

    def gate_cols(lo, hi):
        return _dot(xb, wg_ref[:, lo:hi]) + bg_ref[:, lo:hi]

    W = CONV_WIDTH
    uc = jax.nn.silu(_layer_norm(conv_sc[...], cg_ref[...], cb_ref[...])) * jax.nn.silu(gate_cols(0, W))
    y_conv = _dot(uc.astype(_BF16), wc_ref[...]) + bc_ref[...]
    oa = o_ref[...] * jax.nn.silu(gate_cols(W, 2 * W))
    y_attn = _dot(oa.astype(_BF16), wa_ref[...])
    merged = (jax.nn.sigmoid(gate_cols(2 * W, 2 * W + D_MODEL)) * y_conv
              + jax.nn.sigmoid(gate_cols(2 * W + D_MODEL, 2 * W + 2 * D_MODEL)) * y_attn)
    out = _dot(merged.astype(_BF16), wo_ref[...]) + bo_ref[...]
    y_ref[...] = _layer_norm(alpha * x_ref[...] + out, g_ref[...], b_ref[...])


def _out_call(x, u, o, w_g, b_g, dw, dwb, cg, cb, wc, bc, wa, wo, bo, g, b, seq_len, alpha):
    n_tok = x.shape[0]
    tm = OUT_ROWS
    tiles_per_seq = seq_len // tm
    hb = tm // HALO_ROWS
    n_halo = n_tok // HALO_ROWS
    row = lambda i: (i, 0)
    const = lambda i: (0, 0)
    prev = lambda i: (jnp.maximum(i * hb - 1, 0), 0)
    nxt = lambda i: (jnp.minimum((i + 1) * hb, n_halo - 1), 0)

    def resident(arr):
        return pl.BlockSpec(arr.shape, const, pipeline_mode=pl.Buffered(1))

    kernel = functools.partial(_out_kernel, tiles_per_seq=tiles_per_seq, alpha=alpha)
    return pl.pallas_call(
        kernel,
        grid=(n_tok // tm,),
        in_specs=[
            pl.BlockSpec((tm, D_MODEL), row),
            pl.BlockSpec((HALO_ROWS, CONV_WIDTH), prev),
            pl.BlockSpec((tm, CONV_WIDTH), row),
            pl.BlockSpec((HALO_ROWS, CONV_WIDTH), nxt),
            pl.BlockSpec((tm, ATTN_WIDTH), row),
            resident(w_g), resident(b_g), resident(dw), resident(dwb), resident(cg), resident(cb),
            resident(wc), resident(bc), resident(wa), resident(wo), resident(bo), resident(g), resident(b),
        ],
        out_specs=pl.BlockSpec((tm, D_MODEL), row),
        out_shape=jax.ShapeDtypeStruct((n_tok, D_MODEL), _F32),
        scratch_shapes=[
            pltpu.VMEM((tm + 2 * HALO_ROWS, CONV_WIDTH), _F32),
            pltpu.VMEM((tm, CONV_WIDTH), _F32),
        ],
        compiler_params=pltpu.CompilerParams(
            dimension_semantics=("arbitrary",), vmem_limit_bytes=VMEM_LIMIT_BYTES),
        name="out",
    )(x, u, u, u, o, w_g, b_g, dw, dwb, cg, cb, wc, bc, wa, wo, bo, g, b)


def _rope_tables(seq_len):
    pos = jnp.arange(seq_len, dtype=_F32)
    inv_freq = 1.0 / jnp.power(ROPE_THETA, jnp.arange(0, HEAD_DIM, 2, dtype=_F32) / HEAD_DIM)
    ang = pos[:, None] * inv_freq[None, :]
    cos, sin = jnp.cos(ang), jnp.sin(ang)
    zero = jnp.zeros_like(sin)
    cos_t = jnp.concatenate([cos, cos, cos, cos], axis=-1)
    sin_lo = jnp.concatenate([-sin, zero, -sin, zero], axis=-1)
    sin_hi = jnp.concatenate([zero, sin, zero, sin], axis=-1)
    return cos_t, sin_lo, sin_hi


def _encoder_layer(x, seq_len, p, lam_init, alpha):
    cos, sin_lo, sin_hi = _rope_tables(seq_len)
    u, q, k, v = _proj_call(x, p["w_a"], p["b_a"], cos, sin_lo, sin_hi, seq_len)
    o = _attn_call(q, k, v, p["lq1"], p["lk1"], p["lq2"], p["lk2"], p["subln_g"], seq_len, lam_init)
    return _out_call(x, u, o, p["w_g"], p["b_g"], p["dw"], p["dwb"], p["cg"], p["cb"], p["wc"], p["bc"],
                     p["wa"], p["wo"], p["bo"], p["g"], p["b"], seq_len, alpha)


def _layer_params(l, w_in, b_in, conv_dw, conv_dw_b, conv_ln_g, conv_ln_b, w_conv_proj, b_conv_proj,
                  lam_q1, lam_k1, lam_q2, lam_k2, subln_g, w_attn_o, w_out, b_out, ln_g, ln_b):
    W = CONV_WIDTH
    row = lambda a: a[l][None, :]
    w, b = w_in[l], b_in[l]
    a_cols = [slice(0, 2 * W), slice(3 * W, 6 * W)]
    g_cols = [slice(2 * W, 3 * W), slice(6 * W, w.shape[1])]
    return {
        "w_a": jnp.concatenate([w[:, s] for s in a_cols], axis=1).astype(_BF16),
        "b_a": jnp.concatenate([b[s] for s in a_cols])[None, :],
        "w_g": jnp.concatenate([w[:, s] for s in g_cols], axis=1).astype(_BF16),
        "b_g": jnp.concatenate([b[s] for s in g_cols])[None, :],
        "dw": conv_dw[l], "dwb": row(conv_dw_b), "cg": row(conv_ln_g), "cb": row(conv_ln_b),
        "wc": w_conv_proj[l].astype(_BF16), "bc": row(b_conv_proj),
        "lq1": row(lam_q1), "lk1": row(lam_k1), "lq2": row(lam_q2), "lk2": row(lam_k2),
        "subln_g": row(subln_g), "wa": w_attn_o[l].astype(_BF16),
        "wo": w_out[l].astype(_BF16), "bo": row(b_out), "g": row(ln_g), "b": row(ln_b),
    }


def kernel(x_prompt, x_sample, w_in, b_in, conv_dw, conv_dw_b, conv_ln_g, conv_ln_b, w_conv_proj, b_conv_proj,
           lam_q1, lam_k1, lam_q2, lam_k2, subln_g, w_attn_o, w_out, b_out, ln_g, ln_b):
    depth = w_in.shape[0]
    alpha = (2.0 * depth) ** 0.25
    weights = (w_in, b_in, conv_dw, conv_dw_b, conv_ln_g, conv_ln_b, w_conv_proj, b_conv_proj,
               lam_q1, lam_k1, lam_q2, lam_k2, subln_g, w_attn_o, w_out, b_out, ln_g, ln_b)
    ys = []
    for x in (x_prompt, x_sample):
        n_seq, seq_len, d = x.shape
        y = x.reshape(n_seq * seq_len, d)
        for l in range(depth):
            lam_init = 0.8 - 0.6 * math.exp(-0.3 * l)
            y = _encoder_layer(y, seq_len, _layer_params(l, *weights), lam_init, alpha)
        ys.append(y.reshape(n_seq, seq_len, d))
    return tuple(ys)
```

```python
import functools
import math

import jax
import jax.numpy as jnp
from jax import lax
from jax.experimental import pallas as pl
from jax.experimental.pallas import tpu as pltpu

D_MODEL = 1024
CONV_WIDTH = 512
CONV_KERNEL = 31
CONV_PAD = (CONV_KERNEL - 1) // 2
N_HEADS = 4
HEAD_DIM = 64
V_DIM = 2 * HEAD_DIM
V_ONES_WIDTH = 2 * V_DIM
ATTN_WIDTH = N_HEADS * V_DIM
QK_WIDTH = N_HEADS * 2 * HEAD_DIM
ROPE_THETA = 10000.0
LN_EPS = 1e-5

LANES = 128
SUBLANES = 8
HALO_ROWS = 16
VMEM_LIMIT_BYTES = 56 * 1024 * 1024

PROJ_ROWS = 512
OUT_ROWS = 256
CONV_ROW_CHUNK = 32
ATTN_Q_ROWS = 256
ATTN_K_ROWS = 1024

_BF16 = jnp.bfloat16
_F32 = jnp.float32


def _dot(a, b):
    return jnp.dot(a, b, preferred_element_type=_F32)


def _rope(t, cos, sin_lo, sin_hi):
    up = pltpu.roll(t, LANES - HEAD_DIM // 2, 1)
    down = pltpu.roll(t, HEAD_DIM // 2, 1)
    return t * cos + up * sin_lo + down * sin_hi


def _proj_kernel(x_ref, w_ref, b_ref, cos_ref, sinlo_ref, sinhi_ref, u_ref, q_ref, k_ref, v_ref):
    xb = x_ref[...].astype(_BF16)
    W = CONV_WIDTH

    def cols(j):
        return _dot(xb, w_ref[:, j * W:(j + 1) * W]) + b_ref[:, j * W:(j + 1) * W]

    u_ref[...] = cols(0) * jax.nn.sigmoid(cols(1))
    cos, sin_lo, sin_hi = cos_ref[...], sinlo_ref[...], sinhi_ref[...]
    scale = math.log2(math.e) / math.sqrt(HEAD_DIM)
    tq = cols(2)
    tk = cols(3)
    tv = cols(4).astype(_BF16)
    ones = jnp.ones((tv.shape[0], V_DIM), _BF16)
    for h in range(N_HEADS):
        sl = slice(h * V_DIM, (h + 1) * V_DIM)
        q_ref[:, sl] = (_rope(tq[:, sl], cos, sin_lo, sin_hi) * scale).astype(_BF16)
        k_ref[:, sl] = _rope(tk[:, sl], cos, sin_lo, sin_hi).astype(_BF16)
        v_ref[:, h * V_ONES_WIDTH:h * V_ONES_WIDTH + V_DIM] = tv[:, sl]
        v_ref[:, h * V_ONES_WIDTH + V_DIM:(h + 1) * V_ONES_WIDTH] = ones


def _proj_call(x, w_a, b_a, cos, sin_lo, sin_hi, seq_len):
    n_tok = x.shape[0]
    tm = PROJ_ROWS
    n_pos_blocks = seq_len // tm
    const = lambda i: (0, 0)
    row = lambda i: (i, 0)
    pos = lambda i: (i % n_pos_blocks, 0)
    wide = w_a.shape[1]
    return pl.pallas_call(
        _proj_kernel,
        grid=(n_tok // tm,),
        in_specs=[
            pl.BlockSpec((tm, D_MODEL), row),
            pl.BlockSpec((D_MODEL, wide), const, pipeline_mode=pl.Buffered(1)),
            pl.BlockSpec((1, wide), const),
            pl.BlockSpec((tm, LANES), pos),
            pl.BlockSpec((tm, LANES), pos),
            pl.BlockSpec((tm, LANES), pos),
        ],
        out_specs=[
            pl.BlockSpec((tm, CONV_WIDTH), row),
            pl.BlockSpec((tm, QK_WIDTH), row),
            pl.BlockSpec((tm, QK_WIDTH), row),
            pl.BlockSpec((tm, N_HEADS * V_ONES_WIDTH), row),
        ],
        out_shape=[
            jax.ShapeDtypeStruct((n_tok, CONV_WIDTH), _F32),
            jax.ShapeDtypeStruct((n_tok, QK_WIDTH), _BF16),
            jax.ShapeDtypeStruct((n_tok, QK_WIDTH), _BF16),
            jax.ShapeDtypeStruct((n_tok, N_HEADS * V_ONES_WIDTH), _BF16),
        ],
        compiler_params=pltpu.CompilerParams(
            dimension_semantics=("arbitrary",), vmem_limit_bytes=VMEM_LIMIT_BYTES),
        name="proj",
    )(x, w_a, b_a, cos, sin_lo, sin_hi)


def _attn_kernel(lq1_ref, lk1_ref, lq2_ref, lk2_ref, g_ref, q_ref, k_ref, v_ref, o_ref,
                 s_a, s_b, m_sc, acc_sc, *, seq_len, lam_init):
    tq = q_ref.shape[0]
    tk = ATTN_K_ROWS
    n_chunks = seq_len // tk
    q = q_ref[...]
    lane = lax.broadcasted_iota(jnp.int32, q.shape, 1)
    zero = jnp.zeros_like(q)
    qz = jnp.concatenate([jnp.where(lane < HEAD_DIM, q, zero), jnp.where(lane >= HEAD_DIM, q, zero)], axis=0)

    m_sc[...] = jnp.full(m_sc.shape, -jnp.inf, _F32)
    acc_sc[...] = jnp.zeros(acc_sc.shape, _F32)

    def scores(c):
        start = pl.multiple_of(c * tk, tk)
        return lax.dot_general(qz, k_ref[pl.ds(start, tk), :], (((1,), (1,)), ((), ())),
                               preferred_element_type=_F32)

    def absorb(s_ref, c):
        s = s_ref[...]
        m_prev = m_sc[...]
        m_new = jnp.maximum(m_prev, jnp.max(s, axis=1, keepdims=True))
        alpha = jnp.exp2(m_prev - m_new)
        p = jnp.exp2(s - pltpu.repeat(m_new, tk // LANES, axis=1)).astype(_BF16)
        start = pl.multiple_of(c * tk, tk)
        acc_sc[...] = pltpu.repeat(alpha, 2, axis=1) * acc_sc[...] + _dot(p, v_ref[pl.ds(start, tk), :])
        m_sc[...] = m_new

    s_a[...] = scores(0)

    def body(t, carry):
        c = 2 * t
        s_b[...] = scores(c + 1)
        absorb(s_a, c)
        s_a[...] = scores(jnp.minimum(c + 2, n_chunks - 1))
        absorb(s_b, c + 1)
        return carry

    lax.fori_loop(0, n_chunks // 2, body, 0)

    acc = acc_sc[...]
    o_all = acc[:, :V_DIM] / acc[:, V_DIM:]
    lam = (jnp.exp(jnp.sum(lq1_ref[...] * lk1_ref[...], axis=1, keepdims=True))
           - jnp.exp(jnp.sum(lq2_ref[...] * lk2_ref[...], axis=1, keepdims=True)) + lam_init)
    o = o_all[:tq] - lam * o_all[tq:]
    o = o * lax.rsqrt(jnp.mean(jnp.square(o), axis=-1, keepdims=True) + LN_EPS) * g_ref[...]
    o_ref[...] = o * (1.0 - lam_init)


def _attn_call(q, k, v, lq1, lk1, lq2, lk2, subln_g, seq_len, lam_init):
    n_tok = q.shape[0]
    n_seq = n_tok // seq_len
    tq = ATTN_Q_ROWS
    n_qb = seq_len // tq
    small = lambda b, h, i: (0, 0)
    qmap = lambda b, h, i: (b * n_qb + i, h)
    kvmap = lambda b, h, i: (b, h)
    kernel = functools.partial(_attn_kernel, seq_len=seq_len, lam_init=lam_init)
    return pl.pallas_call(
        kernel,
        grid=(n_seq, N_HEADS, n_qb),
        in_specs=[
            pl.BlockSpec((1, HEAD_DIM), small),
            pl.BlockSpec((1, HEAD_DIM), small),
            pl.BlockSpec((1, HEAD_DIM), small),
            pl.BlockSpec((1, HEAD_DIM), small),
            pl.BlockSpec((1, V_DIM), small),
            pl.BlockSpec((tq, V_DIM), qmap),
            pl.BlockSpec((seq_len, V_DIM), kvmap),
            pl.BlockSpec((seq_len, V_ONES_WIDTH), kvmap),
        ],
        out_specs=pl.BlockSpec((tq, V_DIM), qmap),
        out_shape=jax.ShapeDtypeStruct((n_tok, ATTN_WIDTH), _F32),
        scratch_shapes=[
            pltpu.VMEM((2 * tq, ATTN_K_ROWS), _F32),
            pltpu.VMEM((2 * tq, ATTN_K_ROWS), _F32),
            pltpu.VMEM((2 * tq, LANES), _F32),
            pltpu.VMEM((2 * tq, V_ONES_WIDTH), _F32),
        ],
        compiler_params=pltpu.CompilerParams(
            dimension_semantics=("arbitrary", "arbitrary", "arbitrary"), vmem_limit_bytes=VMEM_LIMIT_BYTES),
        name="attn",
    )(lq1, lk1, lq2, lk2, subln_g, q, k, v)


def _layer_norm(z, g, b):
    mu = jnp.mean(z, axis=-1, keepdims=True)
    zc = z - mu
    var = jnp.mean(jnp.square(zc), axis=-1, keepdims=True)
    return zc * lax.rsqrt(var + LN_EPS) * g + b


def _out_kernel(x_ref, up_ref, uc_ref, un_ref, o_ref, wg_ref, bg_ref, dw_ref, dwb_ref, cg_ref, cb_ref,
                wc_ref, bc_ref, wa_ref, wo_ref, bo_ref, g_ref, b_ref, y_ref, ubuf, conv_sc,
                *, tiles_per_seq, alpha):
    tm = x_ref.shape[0]
    i = pl.program_id(0)
    pos = i % tiles_per_seq
    ubuf[0:HALO_ROWS, :] = jnp.where(pos > 0, up_ref[...], 0.0)
    ubuf[HALO_ROWS:HALO_ROWS + tm, :] = uc_ref[...]
    ubuf[HALO_ROWS + tm:2 * HALO_ROWS + tm, :] = jnp.where(pos < tiles_per_seq - 1, un_ref[...], 0.0)

    rc = CONV_ROW_CHUNK
    base = HALO_ROWS - CONV_PAD
    for r in range(0, tm, rc):
        acc = jnp.broadcast_to(dwb_ref[...], (rc, CONV_WIDTH))
        for t in range(CONV_KERNEL):
            acc = acc + ubuf[base + r + t:base + r + t + rc, :] * dw_ref[t:t + 1, :]
        conv_sc[r:r + rc, :] = acc

    xb = x_ref[...].astype(_BF16)

    def gate_cols(lo, hi):
        return _dot(xb, wg_ref[:, lo:hi]) + bg_ref[:, lo:hi]

    W = CONV_WIDTH
    uc = jax.nn.silu(_layer_norm(conv_sc[...], cg_ref[...], cb_ref[...])) * jax.nn.silu(gate_cols(0, W))
    y_conv = _dot(uc.astype(_BF16), wc_ref[...]) + bc_ref[...]
    oa = o_ref[...] * jax.nn.silu(gate_cols(W, 2 * W))
    y_attn = _dot(oa.astype(_BF16), wa_ref[...])
    merged = (jax.nn.sigmoid(gate_cols(2 * W, 2 * W + D_MODEL)) * y_conv
              + jax.nn.sigmoid(gate_cols(2 * W + D_MODEL, 2 * W + 2 * D_MODEL)) * y_attn)
    out = _dot(merged.astype(_BF16), wo_ref[...]) + bo_ref[...]
    y_ref[...] = _layer_norm(alpha * x_ref[...] + out, g_ref[...], b_ref[...])


def _out_call(x, u, o, w_g, b_g, dw, dwb, cg, cb, wc, bc, wa, wo, bo, g, b, seq_len, alpha):
    n_tok = x.shape[0]
    tm = OUT_ROWS
    tiles_per_seq = seq_len // tm
    hb = tm // HALO_ROWS
    n_halo = n_tok // HALO_ROWS
    row = lambda i: (i, 0)
    const = lambda i: (0, 0)
    prev = lambda i: (jnp.maximum(i * hb - 1, 0), 0)
    nxt = lambda i: (jnp.minimum((i + 1) * hb, n_halo - 1), 0)

    def resident(arr):
        return pl.BlockSpec(arr.shape, const, pipeline_mode=pl.Buffered(1))

    kernel = functools.partial(_out_kernel, tiles_per_seq=tiles_per_seq, alpha=alpha)
    return pl.pallas_call(
        kernel,
        grid=(n_tok // tm,),
        in_specs=[
            pl.BlockSpec((tm, D_MODEL), row),
            pl.BlockSpec((HALO_ROWS, CONV_WIDTH), prev),
            pl.BlockSpec((tm, CONV_WIDTH), row),
            pl.BlockSpec((HALO_ROWS, CONV_WIDTH), nxt),
            pl.BlockSpec((tm, ATTN_WIDTH), row),
            resident(w_g), resident(b_g), resident(dw), resident(dwb), resident(cg), resident(cb),
            resident(wc), resident(bc), resident(wa), resident(wo), resident(bo), resident(g), resident(b),
        ],
        out_specs=pl.BlockSpec((tm, D_MODEL), row),
        out_shape=jax.ShapeDtypeStruct((n_tok, D_MODEL), _F32),
        scratch_shapes=[
            pltpu.VMEM((tm + 2 * HALO_ROWS, CONV_WIDTH), _F32),
            pltpu.VMEM((tm, CONV_WIDTH), _F32),
        ],
        compiler_params=pltpu.CompilerParams(
            dimension_semantics=("arbitrary",), vmem_limit_bytes=VMEM_LIMIT_BYTES),
        name="out",
    )(x, u, u, u, o, w_g, b_g, dw, dwb, cg, cb, wc, bc, wa, wo, bo, g, b)


def _rope_tables(seq_len):
    pos = jnp.arange(seq_len, dtype=_F32)
    inv_freq = 1.0 / jnp.power(ROPE_THETA, jnp.arange(0, HEAD_DIM, 2, dtype=_F32) / HEAD_DIM)
    ang = pos[:, None] * inv_freq[None, :]
    cos, sin = jnp.cos(ang), jnp.sin(ang)
    zero = jnp.zeros_like(sin)
    cos_t = jnp.concatenate([cos, cos, cos, cos], axis=-1)
    sin_lo = jnp.concatenate([-sin, zero, -sin, zero], axis=-1)
    sin_hi = jnp.concatenate([zero, sin, zero, sin], axis=-1)
    return cos_t, sin_lo, sin_hi


def _encoder_layer(x, seq_len, p, lam_init, alpha):
    cos, sin_lo, sin_hi = _rope_tables(seq_len)
    u, q, k, v = _proj_call(x, p["w_a"], p["b_a"], cos, sin_lo, sin_hi, seq_len)
    o = _attn_call(q, k, v, p["lq1"], p["lk1"], p["lq2"], p["lk2"], p["subln_g"], seq_len, lam_init)
    return _out_call(x, u, o, p["w_g"], p["b_g"], p["dw"], p["dwb"], p["cg"], p["cb"], p["wc"], p["bc"],
                     p["wa"], p["wo"], p["bo"], p["g"], p["b"], seq_len, alpha)


def _layer_params(l, w_in, b_in, conv_dw, conv_dw_b, conv_ln_g, conv_ln_b, w_conv_proj, b_conv_proj,
                  lam_q1, lam_k1, lam_q2, lam_k2, subln_g, w_attn_o, w_out, b_out, ln_g, ln_b):
    W = CONV_WIDTH
    row = lambda a: a[l][None, :]
    w, b = w_in[l], b_in[l]
    a_cols = [slice(0, 2 * W), slice(3 * W, 6 * W)]
    g_cols = [slice(2 * W, 3 * W), slice(6 * W, w.shape[1])]
    return {
        "w_a": jnp.concatenate([w[:, s] for s in a_cols], axis=1).astype(_BF16),
        "b_a": jnp.concatenate([b[s] for s in a_cols])[None, :],
        "w_g": jnp.concatenate([w[:, s] for s in g_cols], axis=1).astype(_BF16),
        "b_g": jnp.concatenate([b[s] for s in g_cols])[None, :],
        "dw": conv_dw[l], "dwb": row(conv_dw_b), "cg": row(conv_ln_g), "cb": row(conv_ln_b),
        "wc": w_conv_proj[l].astype(_BF16), "bc": row(b_conv_proj),
        "lq1": row(lam_q1), "lk1": row(lam_k1), "lq2": row(lam_q2), "lk2": row(lam_k2),
        "subln_g": row(subln_g), "wa": w_attn_o[l].astype(_BF16),
        "wo": w_out[l].astype(_BF16), "bo": row(b_out), "g": row(ln_g), "b": row(ln_b),
    }


def kernel(x_prompt, x_sample, w_in, b_in, conv_dw, conv_dw_b, conv_ln_g, conv_ln_b, w_conv_proj, b_conv_proj,
           lam_q1, lam_k1, lam_q2, lam_k2, subln_g, w_attn_o, w_out, b_out, ln_g, ln_b):
    depth = w_in.shape[0]
    alpha = (2.0 * depth) ** 0.25
    weights = (w_in, b_in, conv_dw, conv_dw_b, conv_ln_g, conv_ln_b, w_conv_proj, b_conv_proj,
               lam_q1, lam_k1, lam_q2, lam_k2, subln_g, w_attn_o, w_out, b_out, ln_g, ln_b)
    ys = []
    for x in (x_prompt, x_sample):
        n_seq, seq_len, d = x.shape
        y = x.reshape(n_seq * seq_len, d)
        for l in range(depth):
            lam_init = 0.8 - 0.6 * math.exp(-0.3 * l)
            y = _encoder_layer(y, seq_len, _layer_params(l, *weights), lam_init, alpha)
        ys.append(y.reshape(n_seq, seq_len, d))
    return tuple(ys)
```

```python
import functools
import math

import jax
import jax.numpy as jnp
from jax import lax
from jax.experimental import pallas as pl
from jax.experimental.pallas import tpu as pltpu

D_MODEL = 1024
CONV_WIDTH = 512
CONV_KERNEL = 31
CONV_PAD = (CONV_KERNEL - 1) // 2
N_HEADS = 4
HEAD_DIM = 64
V_DIM = 2 * HEAD_DIM
V_ONES_WIDTH = 2 * V_DIM
ATTN_WIDTH = N_HEADS * V_DIM
QK_WIDTH = N_HEADS * 2 * HEAD_DIM
ROPE_THETA = 10000.0
LN_EPS = 1e-5

LANES = 128
SUBLANES = 8
HALO_ROWS = 16
VMEM_LIMIT_BYTES = 56 * 1024 * 1024

PROJ_ROWS = 512
OUT_ROWS = 256
CONV_ROW_CHUNK = 32
ATTN_Q_ROWS = 512
ATTN_K_ROWS = 1024

_BF16 = jnp.bfloat16
_F32 = jnp.float32


def _dot(a, b):
    return jnp.dot(a, b, preferred_element_type=_F32)


def _rope(t, cos, sin_lo, sin_hi):
    up = pltpu.roll(t, LANES - HEAD_DIM // 2, 1)
    down = pltpu.roll(t, HEAD_DIM // 2, 1)
    return t * cos + up * sin_lo + down * sin_hi


def _proj_kernel(x_ref, w_ref, b_ref, cos_ref, sinlo_ref, sinhi_ref, u_ref, q_ref, k_ref, v_ref):
    xb = x_ref[...].astype(_BF16)
    W = CONV_WIDTH

    def cols(j):
        return _dot(xb, w_ref[:, j * W:(j + 1) * W]) + b_ref[:, j * W:(j + 1) * W]

    u_ref[...] = cols(0) * jax.nn.sigmoid(cols(1))
    cos, sin_lo, sin_hi = cos_ref[...], sinlo_ref[...], sinhi_ref[...]
    scale = math.log2(math.e) / math.sqrt(HEAD_DIM)
    tq = cols(2)
    tk = cols(3)
    tv = cols(4).astype(_BF16)
    ones = jnp.ones((tv.shape[0], V_DIM), _BF16)
    for h in range(N_HEADS):
        sl = slice(h * V_DIM, (h + 1) * V_DIM)
        q_ref[:, sl] = (_rope(tq[:, sl], cos, sin_lo, sin_hi) * scale).astype(_BF16)
        k_ref[:, sl] = _rope(tk[:, sl], cos, sin_lo, sin_hi).astype(_BF16)
        v_ref[:, h * V_ONES_WIDTH:h * V_ONES_WIDTH + V_DIM] = tv[:, sl]
        v_ref[:, h * V_ONES_WIDTH + V_DIM:(h + 1) * V_ONES_WIDTH] = ones


def _proj_call(x, w_a, b_a, cos, sin_lo, sin_hi, seq_len):
    n_tok = x.shape[0]
    tm = PROJ_ROWS
    n_pos_blocks = seq_len // tm
    const = lambda i: (0, 0)
    row = lambda i: (i, 0)
    pos = lambda i: (i % n_pos_blocks, 0)
    wide = w_a.shape[1]
    return pl.pallas_call(
        _proj_kernel,
        grid=(n_tok // tm,),
        in_specs=[
            pl.BlockSpec((tm, D_MODEL), row),
            pl.BlockSpec((D_MODEL, wide), const, pipeline_mode=pl.Buffered(1)),
            pl.BlockSpec((1, wide), const),
            pl.BlockSpec((tm, LANES), pos),
            pl.BlockSpec((tm, LANES), pos),
            pl.BlockSpec((tm, LANES), pos),
        ],
        out_specs=[
            pl.BlockSpec((tm, CONV_WIDTH), row),
            pl.BlockSpec((tm, QK_WIDTH), row),
            pl.BlockSpec((tm, QK_WIDTH), row),
            pl.BlockSpec((tm, N_HEADS * V_ONES_WIDTH), row),
        ],
        out_shape=[
            jax.ShapeDtypeStruct((n_tok, CONV_WIDTH), _F32),
            jax.ShapeDtypeStruct((n_tok, QK_WIDTH), _BF16),
            jax.ShapeDtypeStruct((n_tok, QK_WIDTH), _BF16),
            jax.ShapeDtypeStruct((n_tok, N_HEADS * V_ONES_WIDTH), _BF16),
        ],
        compiler_params=pltpu.CompilerParams(
            dimension_semantics=("arbitrary",), vmem_limit_bytes=VMEM_LIMIT_BYTES),
        name="proj",
    )(x, w_a, b_a, cos, sin_lo, sin_hi)


def _attn_kernel(lq1_ref, lk1_ref, lq2_ref, lk2_ref, g_ref, q_ref, k_ref, v_ref, o_ref,
                 s_a, s_b, m_sc, acc_sc, *, seq_len, lam_init):
    tq = q_ref.shape[0]
    tk = ATTN_K_ROWS
    n_chunks = seq_len // tk
    q = q_ref[...]
    lane = lax.broadcasted_iota(jnp.int32, q.shape, 1)
    zero = jnp.zeros_like(q)
    qz = jnp.concatenate([jnp.where(lane < HEAD_DIM, q, zero), jnp.where(lane >= HEAD_DIM, q, zero)], axis=0)

    m_sc[...] = jnp.full(m_sc.shape, -jnp.inf, _F32)
    acc_sc[...] = jnp.zeros(acc_sc.shape, _F32)

    def scores(c):
        start = pl.multiple_of(c * tk, tk)
        return lax.dot_general(qz, k_ref[pl.ds(start, tk), :], (((1,), (1,)), ((), ())),
                               preferred_element_type=_F32)

    def absorb(s_ref, c):
        s = s_ref[...]
        m_prev = m_sc[...]
        m_new = jnp.maximum(m_prev, jnp.max(s, axis=1, keepdims=True))
        alpha = jnp.exp2(m_prev - m_new)
        p = jnp.exp2(s - pltpu.repeat(m_new, tk // LANES, axis=1)).astype(_BF16)
        start = pl.multiple_of(c * tk, tk)
        acc_sc[...] = pltpu.repeat(alpha, 2, axis=1) * acc_sc[...] + _dot(p, v_ref[pl.ds(start, tk), :])
        m_sc[...] = m_new

    s_a[...] = scores(0)

    def body(t, carry):
        c = 2 * t
        s_b[...] = scores(c + 1)
        absorb(s_a, c)
        s_a[...] = scores(jnp.minimum(c + 2, n_chunks - 1))
        absorb(s_b, c + 1)
        return carry

    lax.fori_loop(0, n_chunks // 2, body, 0)

    acc = acc_sc[...]
    o_all = acc[:, :V_DIM] / acc[:, V_DIM:]
    lam = (jnp.exp(jnp.sum(lq1_ref[...] * lk1_ref[...], axis=1, keepdims=True))
           - jnp.exp(jnp.sum(lq2_ref[...] * lk2_ref[...], axis=1, keepdims=True)) + lam_init)
    o = o_all[:tq] - lam * o_all[tq:]
    o = o * lax.rsqrt(jnp.mean(jnp.square(o), axis=-1, keepdims=True) + LN_EPS) * g_ref[...]
    o_ref[...] = o * (1.0 - lam_init)


def _attn_call(q, k, v, lq1, lk1, lq2, lk2, subln_g, seq_len, lam_init):
    n_tok = q.shape[0]
    n_seq = n_tok // seq_len
    tq = ATTN_Q_ROWS
    n_qb = seq_len // tq
    small = lambda b, h, i: (0, 0)
    qmap = lambda b, h, i: (b * n_qb + i, h)
    kvmap = lambda b, h, i: (b, h)
    kernel = functools.partial(_attn_kernel, seq_len=seq_len, lam_init=lam_init)
    return pl.pallas_call(
        kernel,
        grid=(n_seq, N_HEADS, n_qb),
        in_specs=[
            pl.BlockSpec((1, HEAD_DIM), small),
            pl.BlockSpec((1, HEAD_DIM), small),
            pl.BlockSpec((1, HEAD_DIM), small),
            pl.BlockSpec((1, HEAD_DIM), small),
            pl.BlockSpec((1, V_DIM), small),
            pl.BlockSpec((tq, V_DIM), qmap),
            pl.BlockSpec((seq_len, V_DIM), kvmap),
            pl.BlockSpec((seq_len, V_ONES_WIDTH), kvmap),
        ],
        out_specs=pl.BlockSpec((tq, V_DIM), qmap),
        out_shape=jax.ShapeDtypeStruct((n_tok, ATTN_WIDTH), _F32),
        scratch_shapes=[
            pltpu.VMEM((2 * tq, ATTN_K_ROWS), _F32),
            pltpu.VMEM((2 * tq, ATTN_K_ROWS), _F32),
            pltpu.VMEM((2 * tq, LANES), _F32),
            pltpu.VMEM((2 * tq, V_ONES_WIDTH), _F32),
        ],
        compiler_params=pltpu.CompilerParams(
            dimension_semantics=("arbitrary", "arbitrary", "arbitrary"), vmem_limit_bytes=VMEM_LIMIT_BYTES),
        name="attn",
    )(lq1, lk1, lq2, lk2, subln_g, q, k, v)


def _layer_norm(z, g, b):
    mu = jnp.mean(z, axis=-1, keepdims=True)
    zc = z - mu
    var = jnp.mean(jnp.square(zc), axis=-1, keepdims=True)
    return zc * lax.rsqrt(var + LN_EPS) * g + b


def _out_kernel(x_ref, up_ref, uc_ref, un_ref, o_ref, wg_ref, bg_ref, dw_ref, dwb_ref, cg_ref, cb_ref,
                wc_ref, bc_ref, wa_ref, wo_ref, bo_ref, g_ref, b_ref, y_ref, ubuf, ushift, conv_sc,
                *, tiles_per_seq, alpha):
    tm = x_ref.shape[0]
    i = pl.program_id(0)
    pos = i % tiles_per_seq
    ubuf[0:HALO_ROWS, :] = jnp.where(pos > 0, up_ref[...], 0.0)
    ubuf[HALO_ROWS:HALO_ROWS + tm, :] = uc_ref[...]
    ubuf[HALO_ROWS + tm:2 * HALO_ROWS + tm, :] = jnp.where(pos < tiles_per_seq - 1, un_ref[...], 0.0)

    span = ushift.shape[1]
    for b in range(1, SUBLANES):
        ushift[b - 1, :, :] = ubuf[b:b + span, :]

    rc = CONV_ROW_CHUNK
    base = HALO_ROWS - CONV_PAD
    for r in range(0, tm, rc):
        acc = jnp.broadcast_to(dwb_ref[...], (rc, CONV_WIDTH))
        for t in range(CONV_KERNEL):
            a, b = divmod(base + t, SUBLANES)
            lo = r + a * SUBLANES
            window = ubuf[lo:lo + rc, :] if b == 0 else ushift[b - 1, lo:lo + rc, :]
            acc = acc + window * dw_ref[t:t + 1, :]
        conv_sc[r:r + rc, :] = acc

    xb = x_ref[...].astype(_BF16)

    def gate_cols(lo, hi):
        return _dot(xb, wg_ref[:, lo:hi]) + bg_ref[:, lo:hi]

    W = CONV_WIDTH
    uc = jax.nn.silu(_layer_norm(conv_sc[...], cg_ref[...], cb_ref[...])) * jax.nn.silu(gate_cols(0, W))
    y_conv = _dot(uc.astype(_BF16), wc_ref[...]) + bc_ref[...]
    oa = o_ref[...] * jax.nn.silu(gate_cols(W, 2 * W))
    y_attn = _dot(oa.astype(_BF16), wa_ref[...])
    merged = (jax.nn.sigmoid(gate_cols(2 * W, 2 * W + D_MODEL)) * y_conv
              + jax.nn.sigmoid(gate_cols(2 * W + D_MODEL, 2 * W + 2 * D_MODEL)) * y_attn)
    out = _dot(merged.astype(_BF16), wo_ref[...]) + bo_ref[...]
    y_ref[...] = _layer_norm(alpha * x_ref[...] + out, g_ref[...], b_ref[...])


def _out_call(x, u, o, w_g, b_g, dw, dwb, cg, cb, wc, bc, wa, wo, bo, g, b, seq_len, alpha):
    n_tok = x.shape[0]
    tm = OUT_ROWS
    tiles_per_seq = seq_len // tm
    hb = tm // HALO_ROWS
    n_halo = n_tok // HALO_ROWS
    row = lambda i: (i, 0)
    const = lambda i: (0, 0)
    prev = lambda i: (jnp.maximum(i * hb - 1, 0), 0)
    nxt = lambda i: (jnp.minimum((i + 1) * hb, n_halo - 1), 0)

    def resident(arr):
        return pl.BlockSpec(arr.shape, const, pipeline_mode=pl.Buffered(1))

    kernel = functools.partial(_out_kernel, tiles_per_seq=tiles_per_seq, alpha=alpha)
    return pl.pallas_call(
        kernel,
        grid=(n_tok // tm,),
        in_specs=[
            pl.BlockSpec((tm, D_MODEL), row),
            pl.BlockSpec((HALO_ROWS, CONV_WIDTH), prev),
            pl.BlockSpec((tm, CONV_WIDTH), row),
            pl.BlockSpec((HALO_ROWS, CONV_WIDTH), nxt),
            pl.BlockSpec((tm, ATTN_WIDTH), row),
            resident(w_g), resident(b_g), resident(dw), resident(dwb), resident(cg), resident(cb),
            resident(wc), resident(bc), resident(wa), resident(wo), resident(bo), resident(g), resident(b),
        ],
        out_specs=pl.BlockSpec((tm, D_MODEL), row),
        out_shape=jax.ShapeDtypeStruct((n_tok, D_MODEL), _F32),
        scratch_shapes=[
            pltpu.VMEM((tm + 2 * HALO_ROWS, CONV_WIDTH), _F32),
            pltpu.VMEM((SUBLANES - 1, tm + 2 * HALO_ROWS - SUBLANES, CONV_WIDTH), _F32),
            pltpu.VMEM((tm, CONV_WIDTH), _F32),
        ],
        compiler_params=pltpu.CompilerParams(
            dimension_semantics=("arbitrary",), vmem_limit_bytes=VMEM_LIMIT_BYTES),
        name="out",
    )(x, u, u, u, o, w_g, b_g, dw, dwb, cg, cb, wc, bc, wa, wo, bo, g, b)


def _rope_tables(seq_len):
    half = HEAD_DIM // 2
    pos = jnp.arange(seq_len, dtype=_F32)
    inv_freq = 1.0 / jnp.power(ROPE_THETA, jnp.arange(0, HEAD_DIM, 2, dtype=_F32) / HEAD_DIM)
    ang = pos[:, None] * jnp.tile(inv_freq, LANES // half)[None, :]
    first_half = (jnp.arange(LANES) % HEAD_DIM < half)[None, :]
    sin = jnp.sin(ang)
    return jnp.cos(ang), jnp.where(first_half, -sin, 0.0), jnp.where(first_half, 0.0, sin)


def _encoder_layer(x, seq_len, rope, p, lam_init, alpha):
    cos, sin_lo, sin_hi = rope
    u, q, k, v = _proj_call(x, p["w_a"], p["b_a"], cos, sin_lo, sin_hi, seq_len)
    o = _attn_call(q, k, v, p["lq1"], p["lk1"], p["lq2"], p["lk2"], p["subln_g"], seq_len, lam_init)
    return _out_call(x, u, o, p["w_g"], p["b_g"], p["dw"], p["dwb"], p["cg"], p["cb"], p["wc"], p["bc"],
                     p["wa"], p["wo"], p["bo"], p["g"], p["b"], seq_len, alpha)


def _layer_params(l, w_in, b_in, conv_dw, conv_dw_b, conv_ln_g, conv_ln_b, w_conv_proj, b_conv_proj,
                  lam_q1, lam_k1, lam_q2, lam_k2, subln_g, w_attn_o, w_out, b_out, ln_g, ln_b):
    W = CONV_WIDTH
    row = lambda a: a[l][None, :]
    w, b = w_in[l], b_in[l]
    a_cols = [slice(0, 2 * W), slice(3 * W, 6 * W)]
    g_cols = [slice(2 * W, 3 * W), slice(6 * W, w.shape[1])]
    return {
        "w_a": jnp.concatenate([w[:, s] for s in a_cols], axis=1).astype(_BF16),
        "b_a": jnp.concatenate([b[s] for s in a_cols])[None, :],
        "w_g": jnp.concatenate([w[:, s] for s in g_cols], axis=1).astype(_BF16),
        "b_g": jnp.concatenate([b[s] for s in g_cols])[None, :],
        "dw": conv_dw[l], "dwb": row(conv_dw_b), "cg": row(conv_ln_g), "cb": row(conv_ln_b),
        "wc": w_conv_proj[l].astype(_BF16), "bc": row(b_conv_proj),
        "lq1": row(lam_q1), "lk1": row(lam_k1), "lq2": row(lam_q2), "lk2": row(lam_k2),
        "subln_g": row(subln_g), "wa": w_attn_o[l].astype(_BF16),
        "wo": w_out[l].astype(_BF16), "bo": row(b_out), "g": row(ln_g), "b": row(ln_b),
    }


def kernel(x_prompt, x_sample, w_in, b_in, conv_dw, conv_dw_b, conv_ln_g, conv_ln_b, w_conv_proj, b_conv_proj,
           lam_q1, lam_k1, lam_q2, lam_k2, subln_g, w_attn_o, w_out, b_out, ln_g, ln_b):
    depth = w_in.shape[0]
    alpha = (2.0 * depth) ** 0.25
    weights = (w_in, b_in, conv_dw, conv_dw_b, conv_ln_g, conv_ln_b, w_conv_proj, b_conv_proj,
               lam_q1, lam_k1, lam_q2, lam_k2, subln_g, w_attn_o, w_out, b_out, ln_g, ln_b)
    rope = _rope_tables(max(x_prompt.shape[1], x_sample.shape[1]))
    params = [_layer_params(l, *weights) for l in range(depth)]
    ys = []
    for x in (x_prompt, x_sample):
        n_seq, seq_len, d = x.shape
        y = x.reshape(n_seq * seq_len, d)
        for l in range(depth):
            lam_init = 0.8 - 0.6 * math.exp(-0.3 * l)
            y = _encoder_layer(y, seq_len, rope, params[l], lam_init, alpha)
        ys.append(y.reshape(n_seq, seq_len, d))
    return tuple(ys)
```

```python
import functools
import math

import jax
import jax.numpy as jnp
from jax import lax
from jax.experimental import pallas as pl
from jax.experimental.pallas import tpu as pltpu

D_MODEL = 1024
CONV_WIDTH = 512
CONV_KERNEL = 31
CONV_PAD = (CONV_KERNEL - 1) // 2
N_HEADS = 4
HEAD_DIM = 64
V_DIM = 2 * HEAD_DIM
V_ONES_WIDTH = 2 * V_DIM
ATTN_WIDTH = N_HEADS * V_DIM
QK_WIDTH = N_HEADS * 2 * HEAD_DIM
ROPE_THETA = 10000.0
LN_EPS = 1e-5

LANES = 128
SUBLANES = 8
HALO_ROWS = 16
VMEM_LIMIT_BYTES = 56 * 1024 * 1024

PROJ_ROWS = 512
OUT_ROWS = 512
CONV_ROW_CHUNK = 32
ATTN_Q_ROWS = 512
ATTN_K_ROWS = 1024

_BF16 = jnp.bfloat16
_F32 = jnp.float32


def _dot(a, b):
    return jnp.dot(a, b, preferred_element_type=_F32)


def _rope(t, cos, sin_lo, sin_hi):
    up = pltpu.roll(t, LANES - HEAD_DIM // 2, 1)
    down = pltpu.roll(t, HEAD_DIM // 2, 1)
    return t * cos + up * sin_lo + down * sin_hi


def _proj_kernel(x_ref, w_ref, b_ref, cos_ref, sinlo_ref, sinhi_ref, u_ref, q_ref, k_ref, v_ref):
    xb = x_ref[...].astype(_BF16)
    W = CONV_WIDTH

    def cols(j):
        return _dot(xb, w_ref[:, j * W:(j + 1) * W]) + b_ref[:, j * W:(j + 1) * W]

    u_ref[...] = cols(0) * jax.nn.sigmoid(cols(1))
    cos, sin_lo, sin_hi = cos_ref[...], sinlo_ref[...], sinhi_ref[...]
    scale = math.log2(math.e) / math.sqrt(HEAD_DIM)
    tq = cols(2)
    tk = cols(3)
    tv = cols(4).astype(_BF16)
    ones = jnp.ones((tv.shape[0], V_DIM), _BF16)
    for h in range(N_HEADS):
        sl = slice(h * V_DIM, (h + 1) * V_DIM)
        q_ref[:, sl] = (_rope(tq[:, sl], cos, sin_lo, sin_hi) * scale).astype(_BF16)
        k_ref[:, sl] = _rope(tk[:, sl], cos, sin_lo, sin_hi).astype(_BF16)
        v_ref[:, h * V_ONES_WIDTH:h * V_ONES_WIDTH + V_DIM] = tv[:, sl]
        v_ref[:, h * V_ONES_WIDTH + V_DIM:(h + 1) * V_ONES_WIDTH] = ones


def _proj_call(x, w_a, b_a, cos, sin_lo, sin_hi, seq_len):
    n_tok = x.shape[0]
    tm = PROJ_ROWS
    n_pos_blocks = seq_len // tm
    const = lambda i: (0, 0)
    row = lambda i: (i, 0)
    pos = lambda i: (i % n_pos_blocks, 0)
    wide = w_a.shape[1]
    return pl.pallas_call(
        _proj_kernel,
        grid=(n_tok // tm,),
        in_specs=[
            pl.BlockSpec((tm, D_MODEL), row),
            pl.BlockSpec((D_MODEL, wide), const, pipeline_mode=pl.Buffered(1)),
            pl.BlockSpec((1, wide), const),
            pl.BlockSpec((tm, LANES), pos),
            pl.BlockSpec((tm, LANES), pos),
            pl.BlockSpec((tm, LANES), pos),
        ],
        out_specs=[
            pl.BlockSpec((tm, CONV_WIDTH), row),
            pl.BlockSpec((tm, QK_WIDTH), row),
            pl.BlockSpec((tm, QK_WIDTH), row),
            pl.BlockSpec((tm, N_HEADS * V_ONES_WIDTH), row),
        ],
        out_shape=[
            jax.ShapeDtypeStruct((n_tok, CONV_WIDTH), _F32),
            jax.ShapeDtypeStruct((n_tok, QK_WIDTH), _BF16),
            jax.ShapeDtypeStruct((n_tok, QK_WIDTH), _BF16),
            jax.ShapeDtypeStruct((n_tok, N_HEADS * V_ONES_WIDTH), _BF16),
        ],
        compiler_params=pltpu.CompilerParams(
            dimension_semantics=("arbitrary",), vmem_limit_bytes=VMEM_LIMIT_BYTES),
        name="proj",
    )(x, w_a, b_a, cos, sin_lo, sin_hi)


def _attn_kernel(lq1_ref, lk1_ref, lq2_ref, lk2_ref, g_ref, q_ref, k_ref, v_ref, o_ref,
                 s_a, s_b, m_sc, acc_sc, *, seq_len, lam_init):
    tq = q_ref.shape[0]
    tk = ATTN_K_ROWS
    n_chunks = seq_len // tk
    q = q_ref[...]
    lane = lax.broadcasted_iota(jnp.int32, q.shape, 1)
    zero = jnp.zeros_like(q)
    qz = jnp.concatenate([jnp.where(lane < HEAD_DIM, q, zero), jnp.where(lane >= HEAD_DIM, q, zero)], axis=0)

    m_sc[...] = jnp.full(m_sc.shape, -jnp.inf, _F32)
    acc_sc[...] = jnp.zeros(acc_sc.shape, _F32)

    def chunk_rows(c):
        start = c * tk
        return pl.ds(start if isinstance(c, int) else pl.multiple_of(start, tk), tk)

    def lane_tiles(x, n):
        return jnp.concatenate([x] * n, axis=1)

    def scores(c):
        return lax.dot_general(qz, k_ref[chunk_rows(c), :], (((1,), (1,)), ((), ())),
                               preferred_element_type=_F32)

    def absorb(s_ref, c):
        s = s_ref[...]
        m_prev = m_sc[...]
        m_new = jnp.maximum(m_prev, jnp.max(s, axis=1, keepdims=True))
        alpha = jnp.exp2(m_prev - m_new)
        p = jnp.exp2(s - lane_tiles(m_new, tk // LANES)).astype(_BF16)
        acc_sc[...] = lane_tiles(alpha, 2) * acc_sc[...] + _dot(p, v_ref[chunk_rows(c), :])
        m_sc[...] = m_new

    s_a[...] = scores(0)

    def body(t, carry):
        c = 2 * t
        s_b[...] = scores(c + 1)
        absorb(s_a, c)
        s_a[...] = scores(c + 2)
        absorb(s_b, c + 1)
        return carry

    lax.fori_loop(0, n_chunks // 2 - 1, body, 0)
    s_b[...] = scores(n_chunks - 1)
    absorb(s_a, n_chunks - 2)
    absorb(s_b, n_chunks - 1)

    acc = acc_sc[...]
    o_all = acc[:, :V_DIM] / acc[:, V_DIM:]
    lam = (jnp.exp(jnp.sum(lq1_ref[...] * lk1_ref[...], axis=1, keepdims=True))
           - jnp.exp(jnp.sum(lq2_ref[...] * lk2_ref[...], axis=1, keepdims=True)) + lam_init)
    o = o_all[:tq] - lam * o_all[tq:]
    o = o * lax.rsqrt(jnp.mean(jnp.square(o), axis=-1, keepdims=True) + LN_EPS) * g_ref[...]
    o_ref[...] = o * (1.0 - lam_init)


def _attn_call(q, k, v, lq1, lk1, lq2, lk2, subln_g, seq_len, lam_init):
    n_tok = q.shape[0]
    n_seq = n_tok // seq_len
    tq = ATTN_Q_ROWS
    n_qb = seq_len // tq
    small = lambda b, h, i: (0, 0)
    qmap = lambda b, h, i: (b * n_qb + i, h)
    kvmap = lambda b, h, i: (b, h)
    kernel = functools.partial(_attn_kernel, seq_len=seq_len, lam_init=lam_init)
    return pl.pallas_call(
        kernel,
        grid=(n_seq, N_HEADS, n_qb),
        in_specs=[
            pl.BlockSpec((1, HEAD_DIM), small),
            pl.BlockSpec((1, HEAD_DIM), small),
            pl.BlockSpec((1, HEAD_DIM), small),
            pl.BlockSpec((1, HEAD_DIM), small),
            pl.BlockSpec((1, V_DIM), small),
            pl.BlockSpec((tq, V_DIM), qmap),
            pl.BlockSpec((seq_len, V_DIM), kvmap),
            pl.BlockSpec((seq_len, V_ONES_WIDTH), kvmap),
        ],
        out_specs=pl.BlockSpec((tq, V_DIM), qmap),
        out_shape=jax.ShapeDtypeStruct((n_tok, ATTN_WIDTH), _F32),
        scratch_shapes=[
            pltpu.VMEM((2 * tq, ATTN_K_ROWS), _F32),
            pltpu.VMEM((2 * tq, ATTN_K_ROWS), _F32),
            pltpu.VMEM((2 * tq, LANES), _F32),
            pltpu.VMEM((2 * tq, V_ONES_WIDTH), _F32),
        ],
        compiler_params=pltpu.CompilerParams(
            dimension_semantics=("arbitrary", "arbitrary", "arbitrary"), vmem_limit_bytes=VMEM_LIMIT_BYTES),
        name="attn",
    )(lq1, lk1, lq2, lk2, subln_g, q, k, v)


def _layer_norm(z, g, b):
    mu = jnp.mean(z, axis=-1, keepdims=True)
    zc = z - mu
    var = jnp.mean(jnp.square(zc), axis=-1, keepdims=True)
    return zc * lax.rsqrt(var + LN_EPS) * g + b


def _out_kernel(x_ref, up_ref, uc_ref, un_ref, o_ref, wg_ref, bg_ref, dw_ref, dwb_ref, cg_ref, cb_ref,
                wc_ref, bc_ref, wa_ref, wo_ref, bo_ref, g_ref, b_ref, y_ref, ubuf, ushift, conv_sc,
                *, tiles_per_seq, alpha):
    tm = x_ref.shape[0]
    i = pl.program_id(0)
    pos = i % tiles_per_seq
    ubuf[0:HALO_ROWS, :] = jnp.where(pos > 0, up_ref[...], 0.0)
    ubuf[HALO_ROWS:HALO_ROWS + tm, :] = uc_ref[...]
    ubuf[HALO_ROWS + tm:2 * HALO_ROWS + tm, :] = jnp.where(pos < tiles_per_seq - 1, un_ref[...], 0.0)

    span = ushift.shape[1]
    for b in range(1, SUBLANES):
        ushift[b - 1, :, :] = ubuf[b:b + span, :]

    rc = CONV_ROW_CHUNK
    base = HALO_ROWS - CONV_PAD
    for r in range(0, tm, rc):
        acc = jnp.broadcast_to(dwb_ref[...], (rc, CONV_WIDTH))
        for t in range(CONV_KERNEL):
            a, b = divmod(base + t, SUBLANES)
            lo = r + a * SUBLANES
            window = ubuf[lo:lo + rc, :] if b == 0 else ushift[b - 1, lo:lo + rc, :]
            acc = acc + window * dw_ref[t:t + 1, :]
        conv_sc[r:r + rc, :] = acc

    xb = x_ref[...].astype(_BF16)

    def gate_cols(lo, hi):
        return _dot(xb, wg_ref[:, lo:hi]) + bg_ref[:, lo:hi]

    W = CONV_WIDTH
    uc = jax.nn.silu(_layer_norm(conv_sc[...], cg_ref[...], cb_ref[...])) * jax.nn.silu(gate_cols(0, W))
    y_conv = _dot(uc.astype(_BF16), wc_ref[...]) + bc_ref[...]
    oa = o_ref[...] * jax.nn.silu(gate_cols(W, 2 * W))
    y_attn = _dot(oa.astype(_BF16), wa_ref[...])
    merged = (jax.nn.sigmoid(gate_cols(2 * W, 2 * W + D_MODEL)) * y_conv
              + jax.nn.sigmoid(gate_cols(2 * W + D_MODEL, 2 * W + 2 * D_MODEL)) * y_attn)
    out = _dot(merged.astype(_BF16), wo_ref[...]) + bo_ref[...]
    y_ref[...] = _layer_norm(alpha * x_ref[...] + out, g_ref[...], b_ref[...])


def _out_call(x, u, o, w_g, b_g, dw, dwb, cg, cb, wc, bc, wa, wo, bo, g, b, seq_len, alpha):
    n_tok = x.shape[0]
    tm = OUT_ROWS
    tiles_per_seq = seq_len // tm
    hb = tm // HALO_ROWS
    n_halo = n_tok // HALO_ROWS
    row = lambda i: (i, 0)
    const = lambda i: (0, 0)
    prev = lambda i: (jnp.maximum(i * hb - 1, 0), 0)
    nxt = lambda i: (jnp.minimum((i + 1) * hb, n_halo - 1), 0)

    def resident(arr):
        return pl.BlockSpec(arr.shape, const, pipeline_mode=pl.Buffered(1))

    kernel = functools.partial(_out_kernel, tiles_per_seq=tiles_per_seq, alpha=alpha)
    return pl.pallas_call(
        kernel,
        grid=(n_tok // tm,),
        in_specs=[
            pl.BlockSpec((tm, D_MODEL), row),
            pl.BlockSpec((HALO_ROWS, CONV_WIDTH), prev),
            pl.BlockSpec((tm, CONV_WIDTH), row),
            pl.BlockSpec((HALO_ROWS, CONV_WIDTH), nxt),
            pl.BlockSpec((tm, ATTN_WIDTH), row),
            resident(w_g), resident(b_g), resident(dw), resident(dwb), resident(cg), resident(cb),
            resident(wc), resident(bc), resident(wa), resident(wo), resident(bo), resident(g), resident(b),
        ],
        out_specs=pl.BlockSpec((tm, D_MODEL), row),
        out_shape=jax.ShapeDtypeStruct((n_tok, D_MODEL), _F32),
        scratch_shapes=[
            pltpu.VMEM((tm + 2 * HALO_ROWS, CONV_WIDTH), _F32),
            pltpu.VMEM((SUBLANES - 1, tm + 2 * HALO_ROWS - SUBLANES, CONV_WIDTH), _F32),
            pltpu.VMEM((tm, CONV_WIDTH), _F32),
        ],
        compiler_params=pltpu.CompilerParams(
            dimension_semantics=("arbitrary",), vmem_limit_bytes=VMEM_LIMIT_BYTES),
        name="out",
    )(x, u, u, u, o, w_g, b_g, dw, dwb, cg, cb, wc, bc, wa, wo, bo, g, b)


def _rope_tables(seq_len):
    half = HEAD_DIM // 2
    pos = jnp.arange(seq_len, dtype=_F32)
    inv_freq = 1.0 / jnp.power(ROPE_THETA, jnp.arange(0, HEAD_DIM, 2, dtype=_F32) / HEAD_DIM)
    ang = pos[:, None] * jnp.tile(inv_freq, LANES // half)[None, :]
    first_half = (jnp.arange(LANES) % HEAD_DIM < half)[None, :]
    sin = jnp.sin(ang)
    return jnp.cos(ang), jnp.where(first_half, -sin, 0.0), jnp.where(first_half, 0.0, sin)


def _encoder_layer(x, seq_len, rope, p, lam_init, alpha):
    cos, sin_lo, sin_hi = rope
    u, q, k, v = _proj_call(x, p["w_a"], p["b_a"], cos, sin_lo, sin_hi, seq_len)
    o = _attn_call(q, k, v, p["lq1"], p["lk1"], p["lq2"], p["lk2"], p["subln_g"], seq_len, lam_init)
    return _out_call(x, u, o, p["w_g"], p["b_g"], p["dw"], p["dwb"], p["cg"], p["cb"], p["wc"], p["bc"],
                     p["wa"], p["wo"], p["bo"], p["g"], p["b"], seq_len, alpha)


def _layer_params(l, w_in, b_in, conv_dw, conv_dw_b, conv_ln_g, conv_ln_b, w_conv_proj, b_conv_proj,
                  lam_q1, lam_k1, lam_q2, lam_k2, subln_g, w_attn_o, w_out, b_out, ln_g, ln_b):
    W = CONV_WIDTH
    row = lambda a: a[l][None, :]
    w, b = w_in[l], b_in[l]
    a_cols = [slice(0, 2 * W), slice(3 * W, 6 * W)]
    g_cols = [slice(2 * W, 3 * W), slice(6 * W, w.shape[1])]
    return {
        "w_a": jnp.concatenate([w[:, s] for s in a_cols], axis=1).astype(_BF16),
        "b_a": jnp.concatenate([b[s] for s in a_cols])[None, :],
        "w_g": jnp.concatenate([w[:, s] for s in g_cols], axis=1).astype(_BF16),
        "b_g": jnp.concatenate([b[s] for s in g_cols])[None, :],
        "dw": conv_dw[l], "dwb": row(conv_dw_b), "cg": row(conv_ln_g), "cb": row(conv_ln_b),
        "wc": w_conv_proj[l].astype(_BF16), "bc": row(b_conv_proj),
        "lq1": row(lam_q1), "lk1": row(lam_k1), "lq2": row(lam_q2), "lk2": row(lam_k2),
        "subln_g": row(subln_g), "wa": w_attn_o[l].astype(_BF16),
        "wo": w_out[l].astype(_BF16), "bo": row(b_out), "g": row(ln_g), "b": row(ln_b),
    }


def kernel(x_prompt, x_sample, w_in, b_in, conv_dw, conv_dw_b, conv_ln_g, conv_ln_b, w_conv_proj, b_conv_proj,
           lam_q1, lam_k1, lam_q2, lam_k2, subln_g, w_attn_o, w_out, b_out, ln_g, ln_b):
    depth = w_in.shape[0]
    alpha = (2.0 * depth) ** 0.25
    weights = (w_in, b_in, conv_dw, conv_dw_b, conv_ln_g, conv_ln_b, w_conv_proj, b_conv_proj,
               lam_q1, lam_k1, lam_q2, lam_k2, subln_g, w_attn_o, w_out, b_out, ln_g, ln_b)
    rope = _rope_tables(max(x_prompt.shape[1], x_sample.shape[1]))
    params = [_layer_params(l, *weights) for l in range(depth)]
    ys = []
    for x in (x_prompt, x_sample):
        n_seq, seq_len, d = x.shape
        y = x.reshape(n_seq * seq_len, d)
        for l in range(depth):
            lam_init = 0.8 - 0.6 * math.exp(-0.3 * l)
            y = _encoder_layer(y, seq_len, rope, params[l], lam_init, alpha)
        ys.append(y.reshape(n_seq, seq_len, d))
    return tuple(ys)
```

```python
import functools
import math

import jax
import jax.numpy as jnp
from jax import lax
from jax.experimental import pallas as pl
from jax.experimental.pallas import tpu as pltpu

D_MODEL = 1024
CONV_WIDTH = 512
CONV_KERNEL = 31
CONV_PAD = (CONV_KERNEL - 1) // 2
N_HEADS = 4
HEAD_DIM = 64
V_DIM = 2 * HEAD_DIM
V_ONES_WIDTH = 2 * V_DIM
ATTN_WIDTH = N_HEADS * V_DIM
QK_WIDTH = N_HEADS * 2 * HEAD_DIM
ROPE_THETA = 10000.0
LN_EPS = 1e-5

LANES = 128
SUBLANES = 8
HALO_ROWS = 16
VMEM_LIMIT_BYTES = 56 * 1024 * 1024

PROJ_ROWS = 512
OUT_ROWS = 512
CONV_ROW_CHUNK = 32
ATTN_Q_ROWS = 512
ATTN_K_ROWS = 1024

_BF16 = jnp.bfloat16
_F32 = jnp.float32


def _dot(a, b):
    return jnp.dot(a, b, preferred_element_type=_F32)


def _rope(t, cos, sin_lo, sin_hi):
    up = pltpu.roll(t, LANES - HEAD_DIM // 2, 1)
    down = pltpu.roll(t, HEAD_DIM // 2, 1)
    return t * cos + up * sin_lo + down * sin_hi


def _proj_kernel(x_ref, w_ref, b_ref, cos_ref, sinlo_ref, sinhi_ref, u_ref, q_ref, k_ref, v_ref):
    xb = x_ref[...].astype(_BF16)
    W = CONV_WIDTH

    def cols(j):
        return _dot(xb, w_ref[:, j * W:(j + 1) * W]) + b_ref[:, j * W:(j + 1) * W]

    u_ref[...] = cols(0) * jax.nn.sigmoid(cols(1))
    cos, sin_lo, sin_hi = cos_ref[...], sinlo_ref[...], sinhi_ref[...]
    scale = math.log2(math.e) / math.sqrt(HEAD_DIM)
    tq = cols(2)
    tk = cols(3)
    tv = cols(4).astype(_BF16)
    ones = jnp.ones((tv.shape[0], V_DIM), _BF16)
    for h in range(N_HEADS):
        sl = slice(h * V_DIM, (h + 1) * V_DIM)
        q_ref[:, sl] = (_rope(tq[:, sl], cos, sin_lo, sin_hi) * scale).astype(_BF16)
        k_ref[:, sl] = _rope(tk[:, sl], cos, sin_lo, sin_hi).astype(_BF16)
        v_ref[:, h * V_ONES_WIDTH:h * V_ONES_WIDTH + V_DIM] = tv[:, sl]
        v_ref[:, h * V_ONES_WIDTH + V_DIM:(h + 1) * V_ONES_WIDTH] = ones


def _proj_call(x, w_a, b_a, cos, sin_lo, sin_hi, seq_len):
    n_tok = x.shape[0]
    tm = PROJ_ROWS
    n_pos_blocks = seq_len // tm
    const = lambda i: (0, 0)
    row = lambda i: (i, 0)
    pos = lambda i: (i % n_pos_blocks, 0)
    wide = w_a.shape[1]
    return pl.pallas_call(
        _proj_kernel,
        grid=(n_tok // tm,),
        in_specs=[
            pl.BlockSpec((tm, D_MODEL), row),
            pl.BlockSpec((D_MODEL, wide), const, pipeline_mode=pl.Buffered(1)),
            pl.BlockSpec((1, wide), const),
            pl.BlockSpec((tm, LANES), pos),
            pl.BlockSpec((tm, LANES), pos),
            pl.BlockSpec((tm, LANES), pos),
        ],
        out_specs=[
            pl.BlockSpec((tm, CONV_WIDTH), row),
            pl.BlockSpec((tm, QK_WIDTH), row),
            pl.BlockSpec((tm, QK_WIDTH), row),
            pl.BlockSpec((tm, N_HEADS * V_ONES_WIDTH), row),
        ],
        out_shape=[
            jax.ShapeDtypeStruct((n_tok, CONV_WIDTH), _F32),
            jax.ShapeDtypeStruct((n_tok, QK_WIDTH), _BF16),
            jax.ShapeDtypeStruct((n_tok, QK_WIDTH), _BF16),
            jax.ShapeDtypeStruct((n_tok, N_HEADS * V_ONES_WIDTH), _BF16),
        ],
        compiler_params=pltpu.CompilerParams(
            dimension_semantics=("arbitrary",), vmem_limit_bytes=VMEM_LIMIT_BYTES),
        name="proj",
    )(x, w_a, b_a, cos, sin_lo, sin_hi)


def _attn_kernel(lq1_ref, lk1_ref, lq2_ref, lk2_ref, g_ref, q_ref, k_ref, v_ref, o_ref,
                 s_a, s_b, m_sc, acc_sc, *, seq_len, lam_init):
    tq = q_ref.shape[0]
    tk = ATTN_K_ROWS
    n_chunks = seq_len // tk
    q = q_ref[...]
    lane = lax.broadcasted_iota(jnp.int32, q.shape, 1)
    zero = jnp.zeros_like(q)
    qz = jnp.concatenate([jnp.where(lane < HEAD_DIM, q, zero), jnp.where(lane >= HEAD_DIM, q, zero)], axis=0)

    m_sc[...] = jnp.full(m_sc.shape, -jnp.inf, _F32)
    acc_sc[...] = jnp.zeros(acc_sc.shape, _F32)

    def chunk_rows(c):
        start = c * tk
        return pl.ds(start if isinstance(c, int) else pl.multiple_of(start, tk), tk)

    def lane_tiles(x, n):
        return jnp.concatenate([x] * n, axis=1)

    def scores(c):
        return lax.dot_general(qz, k_ref[chunk_rows(c), :], (((1,), (1,)), ((), ())),
                               preferred_element_type=_F32)

    def absorb(s_ref, c):
        s = s_ref[...]
        m_prev = m_sc[...]
        m_new = jnp.maximum(m_prev, jnp.max(s, axis=1, keepdims=True))
        alpha = jnp.exp2(m_prev - m_new)
        p = jnp.exp2(s - lane_tiles(m_new, tk // LANES)).astype(_BF16)
        acc_sc[...] = lane_tiles(alpha, 2) * acc_sc[...] + _dot(p, v_ref[chunk_rows(c), :])
        m_sc[...] = m_new

    s_a[...] = scores(0)

    def body(t, carry):
        c = 2 * t
        s_b[...] = scores(c + 1)
        absorb(s_a, c)
        s_a[...] = scores(c + 2)
        absorb(s_b, c + 1)
        return carry

    lax.fori_loop(0, n_chunks // 2 - 1, body, 0, unroll=True)
    s_b[...] = scores(n_chunks - 1)
    absorb(s_a, n_chunks - 2)
    absorb(s_b, n_chunks - 1)

    acc = acc_sc[...]
    o_all = acc[:, :V_DIM] / acc[:, V_DIM:]
    lam = (jnp.exp(jnp.sum(lq1_ref[...] * lk1_ref[...], axis=1, keepdims=True))
           - jnp.exp(jnp.sum(lq2_ref[...] * lk2_ref[...], axis=1, keepdims=True)) + lam_init)
    o = o_all[:tq] - lam * o_all[tq:]
    o = o * lax.rsqrt(jnp.mean(jnp.square(o), axis=-1, keepdims=True) + LN_EPS) * g_ref[...]
    o_ref[...] = o * (1.0 - lam_init)


def _attn_call(q, k, v, lq1, lk1, lq2, lk2, subln_g, seq_len, lam_init):
    n_tok = q.shape[0]
    n_seq = n_tok // seq_len
    tq = ATTN_Q_ROWS
    n_qb = seq_len // tq
    small = lambda b, h, i: (0, 0)
    qmap = lambda b, h, i: (b * n_qb + i, h)
    kvmap = lambda b, h, i: (b, h)
    kernel = functools.partial(_attn_kernel, seq_len=seq_len, lam_init=lam_init)
    return pl.pallas_call(
        kernel,
        grid=(n_seq, N_HEADS, n_qb),
        in_specs=[
            pl.BlockSpec((1, HEAD_DIM), small),
            pl.BlockSpec((1, HEAD_DIM), small),
            pl.BlockSpec((1, HEAD_DIM), small),
            pl.BlockSpec((1, HEAD_DIM), small),
            pl.BlockSpec((1, V_DIM), small),
            pl.BlockSpec((tq, V_DIM), qmap),
            pl.BlockSpec((seq_len, V_DIM), kvmap),
            pl.BlockSpec((seq_len, V_ONES_WIDTH), kvmap),
        ],
        out_specs=pl.BlockSpec((tq, V_DIM), qmap),
        out_shape=jax.ShapeDtypeStruct((n_tok, ATTN_WIDTH), _F32),
        scratch_shapes=[
            pltpu.VMEM((2 * tq, ATTN_K_ROWS), _F32),
            pltpu.VMEM((2 * tq, ATTN_K_ROWS), _F32),
            pltpu.VMEM((2 * tq, LANES), _F32),
            pltpu.VMEM((2 * tq, V_ONES_WIDTH), _F32),
        ],
        compiler_params=pltpu.CompilerParams(
            dimension_semantics=("arbitrary", "arbitrary", "arbitrary"), vmem_limit_bytes=VMEM_LIMIT_BYTES),
        name="attn",
    )(lq1, lk1, lq2, lk2, subln_g, q, k, v)


def _layer_norm(z, g, b):
    mu = jnp.mean(z, axis=-1, keepdims=True)
    zc = z - mu
    var = jnp.mean(jnp.square(zc), axis=-1, keepdims=True)
    return zc * lax.rsqrt(var + LN_EPS) * g + b


def _out_kernel(x_ref, up_ref, uc_ref, un_ref, o_ref, wg_ref, bg_ref, dw_ref, dwb_ref, cg_ref, cb_ref,
                wc_ref, bc_ref, wa_ref, wo_ref, bo_ref, g_ref, b_ref, y_ref, ubuf, ushift, conv_sc,
                *, tiles_per_seq, alpha):
    tm = x_ref.shape[0]
    i = pl.program_id(0)
    pos = i % tiles_per_seq
    ubuf[0:HALO_ROWS, :] = jnp.where(pos > 0, up_ref[...], 0.0)
    ubuf[HALO_ROWS:HALO_ROWS + tm, :] = uc_ref[...]
    ubuf[HALO_ROWS + tm:2 * HALO_ROWS + tm, :] = jnp.where(pos < tiles_per_seq - 1, un_ref[...], 0.0)

    span = ushift.shape[1]
    for b in range(1, SUBLANES):
        ushift[b - 1, :, :] = ubuf[b:b + span, :]

    rc = CONV_ROW_CHUNK
    base = HALO_ROWS - CONV_PAD
    for r in range(0, tm, rc):
        acc = jnp.broadcast_to(dwb_ref[...], (rc, CONV_WIDTH))
        for t in range(CONV_KERNEL):
            a, b = divmod(base + t, SUBLANES)
            lo = r + a * SUBLANES
            window = ubuf[lo:lo + rc, :] if b == 0 else ushift[b - 1, lo:lo + rc, :]
            acc = acc + window * dw_ref[t:t + 1, :]
        conv_sc[r:r + rc, :] = acc

    xb = x_ref[...].astype(_BF16)

    def gate_cols(lo, hi):
        return _dot(xb, wg_ref[:, lo:hi]) + bg_ref[:, lo:hi]

    W = CONV_WIDTH
    uc = jax.nn.silu(_layer_norm(conv_sc[...], cg_ref[...], cb_ref[...])) * jax.nn.silu(gate_cols(0, W))
    y_conv = _dot(uc.astype(_BF16), wc_ref[...]) + bc_ref[...]
    oa = o_ref[...] * jax.nn.silu(gate_cols(W, 2 * W))
    y_attn = _dot(oa.astype(_BF16), wa_ref[...])
    merged = (jax.nn.sigmoid(gate_cols(2 * W, 2 * W + D_MODEL)) * y_conv
              + jax.nn.sigmoid(gate_cols(2 * W + D_MODEL, 2 * W + 2 * D_MODEL)) * y_attn)
    out = _dot(merged.astype(_BF16), wo_ref[...]) + bo_ref[...]
    y_ref[...] = _layer_norm(alpha * x_ref[...] + out, g_ref[...], b_ref[...])


def _out_call(x, u, o, w_g, b_g, dw, dwb, cg, cb, wc, bc, wa, wo, bo, g, b, seq_len, alpha):
    n_tok = x.shape[0]
    tm = OUT_ROWS
    tiles_per_seq = seq_len // tm
    hb = tm // HALO_ROWS
    n_halo = n_tok // HALO_ROWS
    row = lambda i: (i, 0)
    const = lambda i: (0, 0)
    prev = lambda i: (jnp.maximum(i * hb - 1, 0), 0)
    nxt = lambda i: (jnp.minimum((i + 1) * hb, n_halo - 1), 0)

    def resident(arr):
        return pl.BlockSpec(arr.shape, const, pipeline_mode=pl.Buffered(1))

    kernel = functools.partial(_out_kernel, tiles_per_seq=tiles_per_seq, alpha=alpha)
    return pl.pallas_call(
        kernel,
        grid=(n_tok // tm,),
        in_specs=[
            pl.BlockSpec((tm, D_MODEL), row),
            pl.BlockSpec((HALO_ROWS, CONV_WIDTH), prev),
            pl.BlockSpec((tm, CONV_WIDTH), row),
            pl.BlockSpec((HALO_ROWS, CONV_WIDTH), nxt),
            pl.BlockSpec((tm, ATTN_WIDTH), row),
            resident(w_g), resident(b_g), resident(dw), resident(dwb), resident(cg), resident(cb),
            resident(wc), resident(bc), resident(wa), resident(wo), resident(bo), resident(g), resident(b),
        ],
        out_specs=pl.BlockSpec((tm, D_MODEL), row),
        out_shape=jax.ShapeDtypeStruct((n_tok, D_MODEL), _F32),
        scratch_shapes=[
            pltpu.VMEM((tm + 2 * HALO_ROWS, CONV_WIDTH), _F32),
            pltpu.VMEM((SUBLANES - 1, tm + 2 * HALO_ROWS - SUBLANES, CONV_WIDTH), _F32),
            pltpu.VMEM((tm, CONV_WIDTH), _F32),
        ],
        compiler_params=pltpu.CompilerParams(
            dimension_semantics=("arbitrary",), vmem_limit_bytes=VMEM_LIMIT_BYTES),
        name="out",
    )(x, u, u, u, o, w_g, b_g, dw, dwb, cg, cb, wc, bc, wa, wo, bo, g, b)


def _rope_tables(seq_len):
    half = HEAD_DIM // 2
    pos = jnp.arange(seq_len, dtype=_F32)
    inv_freq = 1.0 / jnp.power(ROPE_THETA, jnp.arange(0, HEAD_DIM, 2, dtype=_F32) / HEAD_DIM)
    ang = pos[:, None] * jnp.tile(inv_freq, LANES // half)[None, :]
    first_half = (jnp.arange(LANES) % HEAD_DIM < half)[None, :]
    sin = jnp.sin(ang)
    return jnp.cos(ang), jnp.where(first_half, -sin, 0.0), jnp.where(first_half, 0.0, sin)


def _encoder_layer(x, seq_len, rope, p, lam_init, alpha):
    cos, sin_lo, sin_hi = rope
    u, q, k, v = _proj_call(x, p["w_a"], p["b_a"], cos, sin_lo, sin_hi, seq_len)
    o = _attn_call(q, k, v, p["lq1"], p["lk1"], p["lq2"], p["lk2"], p["subln_g"], seq_len, lam_init)
    return _out_call(x, u, o, p["w_g"], p["b_g"], p["dw"], p["dwb"], p["cg"], p["cb"], p["wc"], p["bc"],
                     p["wa"], p["wo"], p["bo"], p["g"], p["b"], seq_len, alpha)


def _layer_params(l, w_in, b_in, conv_dw, conv_dw_b, conv_ln_g, conv_ln_b, w_conv_proj, b_conv_proj,
                  lam_q1, lam_k1, lam_q2, lam_k2, subln_g, w_attn_o, w_out, b_out, ln_g, ln_b):
    W = CONV_WIDTH
    row = lambda a: a[l][None, :]
    w, b = w_in[l], b_in[l]
    a_cols = [slice(0, 2 * W), slice(3 * W, 6 * W)]
    g_cols = [slice(2 * W, 3 * W), slice(6 * W, w.shape[1])]
    return {
        "w_a": jnp.concatenate([w[:, s] for s in a_cols], axis=1).astype(_BF16),
        "b_a": jnp.concatenate([b[s] for s in a_cols])[None, :],
        "w_g": jnp.concatenate([w[:, s] for s in g_cols], axis=1).astype(_BF16),
        "b_g": jnp.concatenate([b[s] for s in g_cols])[None, :],
        "dw": conv_dw[l], "dwb": row(conv_dw_b), "cg": row(conv_ln_g), "cb": row(conv_ln_b),
        "wc": w_conv_proj[l].astype(_BF16), "bc": row(b_conv_proj),
        "lq1": row(lam_q1), "lk1": row(lam_k1), "lq2": row(lam_q2), "lk2": row(lam_k2),
        "subln_g": row(subln_g), "wa": w_attn_o[l].astype(_BF16),
        "wo": w_out[l].astype(_BF16), "bo": row(b_out), "g": row(ln_g), "b": row(ln_b),
    }


def kernel(x_prompt, x_sample, w_in, b_in, conv_dw, conv_dw_b, conv_ln_g, conv_ln_b, w_conv_proj, b_conv_proj,
           lam_q1, lam_k1, lam_q2, lam_k2, subln_g, w_attn_o, w_out, b_out, ln_g, ln_b):
    depth = w_in.shape[0]
    alpha = (2.0 * depth) ** 0.25
    weights = (w_in, b_in, conv_dw, conv_dw_b, conv_ln_g, conv_ln_b, w_conv_proj, b_conv_proj,
               lam_q1, lam_k1, lam_q2, lam_k2, subln_g, w_attn_o, w_out, b_out, ln_g, ln_b)
    rope = _rope_tables(max(x_prompt.shape[1], x_sample.shape[1]))
    params = [_layer_params(l, *weights) for l in range(depth)]
    ys = []
    for x in (x_prompt, x_sample):
        n_seq, seq_len, d = x.shape
        y = x.reshape(n_seq * seq_len, d)
        for l in range(depth):
            lam_init = 0.8 - 0.6 * math.exp(-0.3 * l)
            y = _encoder_layer(y, seq_len, rope, params[l], lam_init, alpha)
        ys.append(y.reshape(n_seq, seq_len, d))
    return tuple(ys)
```

```python
import functools
import math

import jax
import jax.numpy as jnp
from jax import lax
from jax.experimental import pallas as pl
from jax.experimental.pallas import tpu as pltpu

D_MODEL = 1024
CONV_WIDTH = 512
CONV_KERNEL = 31
CONV_PAD = (CONV_KERNEL - 1) // 2
N_HEADS = 4
HEAD_DIM = 64
V_DIM = 2 * HEAD_DIM
V_ONES_WIDTH = 2 * V_DIM
ATTN_WIDTH = N_HEADS * V_DIM
QK_WIDTH = N_HEADS * 2 * HEAD_DIM
ROPE_THETA = 10000.0
LN_EPS = 1e-5

LANES = 128
SUBLANES = 8
HALO_ROWS = 16
VMEM_LIMIT_BYTES = 56 * 1024 * 1024

PROJ_ROWS = 512
OUT_ROWS = 512
CONV_ROW_CHUNK = 32
ATTN_Q_ROWS = 512
ATTN_K_ROWS = 1024

_BF16 = jnp.bfloat16
_F32 = jnp.float32


def _dot(a, b):
    return jnp.dot(a, b, preferred_element_type=_F32)


def _rope(t, cos, sin_lo, sin_hi):
    up = pltpu.roll(t, LANES - HEAD_DIM // 2, 1)
    down = pltpu.roll(t, HEAD_DIM // 2, 1)
    return t * cos + up * sin_lo + down * sin_hi


def _proj_kernel(x_ref, w_ref, b_ref, cos_ref, sinlo_ref, sinhi_ref, u_ref, q_ref, k_ref, v_ref):
    xb = x_ref[...].astype(_BF16)
    W = CONV_WIDTH

    def cols(j):
        return _dot(xb, w_ref[:, j * W:(j + 1) * W]) + b_ref[:, j * W:(j + 1) * W]

    u_ref[...] = cols(0) * jax.nn.sigmoid(cols(1))
    cos, sin_lo, sin_hi = cos_ref[...], sinlo_ref[...], sinhi_ref[...]
    scale = math.log2(math.e) / math.sqrt(HEAD_DIM)
    tq = cols(2)
    tk = cols(3)
    tv = cols(4).astype(_BF16)
    ones = jnp.ones((tv.shape[0], V_DIM), _BF16)
    for h in range(N_HEADS):
        sl = slice(h * V_DIM, (h + 1) * V_DIM)
        q_ref[:, sl] = (_rope(tq[:, sl], cos, sin_lo, sin_hi) * scale).astype(_BF16)
        k_ref[:, sl] = _rope(tk[:, sl], cos, sin_lo, sin_hi).astype(_BF16)
        v_ref[:, h * V_ONES_WIDTH:h * V_ONES_WIDTH + V_DIM] = tv[:, sl]
        v_ref[:, h * V_ONES_WIDTH + V_DIM:(h + 1) * V_ONES_WIDTH] = ones


def _proj_call(x, w_a, b_a, cos, sin_lo, sin_hi, seq_len):
    n_tok = x.shape[0]
    tm = PROJ_ROWS
    n_pos_blocks = seq_len // tm
    const = lambda i: (0, 0)
    row = lambda i: (i, 0)
    pos = lambda i: (i % n_pos_blocks, 0)
    wide = w_a.shape[1]
    return pl.pallas_call(
        _proj_kernel,
        grid=(n_tok // tm,),
        in_specs=[
            pl.BlockSpec((tm, D_MODEL), row),
            pl.BlockSpec((D_MODEL, wide), const, pipeline_mode=pl.Buffered(1)),
            pl.BlockSpec((1, wide), const),
            pl.BlockSpec((tm, LANES), pos),
            pl.BlockSpec((tm, LANES), pos),
            pl.BlockSpec((tm, LANES), pos),
        ],
        out_specs=[
            pl.BlockSpec((tm, CONV_WIDTH), row),
            pl.BlockSpec((tm, QK_WIDTH), row),
            pl.BlockSpec((tm, QK_WIDTH), row),
            pl.BlockSpec((tm, N_HEADS * V_ONES_WIDTH), row),
        ],
        out_shape=[
            jax.ShapeDtypeStruct((n_tok, CONV_WIDTH), _F32),
            jax.ShapeDtypeStruct((n_tok, QK_WIDTH), _BF16),
            jax.ShapeDtypeStruct((n_tok, QK_WIDTH), _BF16),
            jax.ShapeDtypeStruct((n_tok, N_HEADS * V_ONES_WIDTH), _BF16),
        ],
        compiler_params=pltpu.CompilerParams(
            dimension_semantics=("arbitrary",), vmem_limit_bytes=VMEM_LIMIT_BYTES),
        name="proj",
    )(x, w_a, b_a, cos, sin_lo, sin_hi)


def _attn_kernel(lq1_ref, lk1_ref, lq2_ref, lk2_ref, g_ref, q_ref, k_ref, v_ref, o_ref,
                 s_a, s_b, m_sc, acc_sc, *, seq_len, lam_init):
    tq = q_ref.shape[0]
    tk = ATTN_K_ROWS
    n_chunks = seq_len // tk
    q = q_ref[...]
    lane = lax.broadcasted_iota(jnp.int32, q.shape, 1)
    zero = jnp.zeros_like(q)
    qz = jnp.concatenate([jnp.where(lane < HEAD_DIM, q, zero), jnp.where(lane >= HEAD_DIM, q, zero)], axis=0)

    m_sc[...] = jnp.full(m_sc.shape, -jnp.inf, _F32)
    acc_sc[...] = jnp.zeros(acc_sc.shape, _F32)

    def chunk_rows(c):
        start = c * tk
        return pl.ds(start if isinstance(c, int) else pl.multiple_of(start, tk), tk)

    def lane_tiles(x, n):
        return jnp.concatenate([x] * n, axis=1)

    def scores(c):
        return lax.dot_general(qz, k_ref[chunk_rows(c), :], (((1,), (1,)), ((), ())),
                               preferred_element_type=_F32)

    def absorb(s_ref, c):
        s = s_ref[...]
        m_prev = m_sc[...]
        m_new = jnp.maximum(m_prev, jnp.max(s, axis=1, keepdims=True))
        alpha = jnp.exp2(m_prev - m_new)
        p = jnp.exp2(s - lane_tiles(m_new, tk // LANES)).astype(_BF16)
        acc_sc[...] = lane_tiles(alpha, 2) * acc_sc[...] + _dot(p, v_ref[chunk_rows(c), :])
        m_sc[...] = m_new

    s_a[...] = scores(0)

    def body(t, carry):
        c = 2 * t
        s_b[...] = scores(c + 1)
        absorb(s_a, c)
        s_a[...] = scores(c + 2)
        absorb(s_b, c + 1)
        return carry

    lax.fori_loop(0, n_chunks // 2 - 1, body, 0, unroll=True)
    s_b[...] = scores(n_chunks - 1)
    absorb(s_a, n_chunks - 2)
    absorb(s_b, n_chunks - 1)

    acc = acc_sc[...]
    o_all = acc[:, :V_DIM] / acc[:, V_DIM:]
    lam = (jnp.exp(jnp.sum(lq1_ref[...] * lk1_ref[...], axis=1, keepdims=True))
           - jnp.exp(jnp.sum(lq2_ref[...] * lk2_ref[...], axis=1, keepdims=True)) + lam_init)
    o = o_all[:tq] - lam * o_all[tq:]
    o = o * lax.rsqrt(jnp.mean(jnp.square(o), axis=-1, keepdims=True) + LN_EPS) * g_ref[...]
    o_ref[...] = o * (1.0 - lam_init)


def _attn_call(q, k, v, lq1, lk1, lq2, lk2, subln_g, seq_len, lam_init):
    n_tok = q.shape[0]
    n_seq = n_tok // seq_len
    tq = ATTN_Q_ROWS
    n_qb = seq_len // tq
    small = lambda b, h, i: (0, 0)
    qmap = lambda b, h, i: (b * n_qb + i, h)
    kvmap = lambda b, h, i: (b, h)
    kernel = functools.partial(_attn_kernel, seq_len=seq_len, lam_init=lam_init)
    return pl.pallas_call(
        kernel,
        grid=(n_seq, N_HEADS, n_qb),
        in_specs=[
            pl.BlockSpec((1, HEAD_DIM), small),
            pl.BlockSpec((1, HEAD_DIM), small),
            pl.BlockSpec((1, HEAD_DIM), small),
            pl.BlockSpec((1, HEAD_DIM), small),
            pl.BlockSpec((1, V_DIM), small),
            pl.BlockSpec((tq, V_DIM), qmap),
            pl.BlockSpec((seq_len, V_DIM), kvmap),
            pl.BlockSpec((seq_len, V_ONES_WIDTH), kvmap),
        ],
        out_specs=pl.BlockSpec((tq, V_DIM), qmap),
        out_shape=jax.ShapeDtypeStruct((n_tok, ATTN_WIDTH), _F32),
        scratch_shapes=[
            pltpu.VMEM((2 * tq, ATTN_K_ROWS), _F32),
            pltpu.VMEM((2 * tq, ATTN_K_ROWS), _F32),
            pltpu.VMEM((2 * tq, LANES), _F32),
            pltpu.VMEM((2 * tq, V_ONES_WIDTH), _F32),
        ],
        compiler_params=pltpu.CompilerParams(
            dimension_semantics=("arbitrary", "arbitrary", "arbitrary"), vmem_limit_bytes=VMEM_LIMIT_BYTES),
        name="attn",
    )(lq1, lk1, lq2, lk2, subln_g, q, k, v)


def _layer_norm(z, g, b):
    mu = jnp.mean(z, axis=-1, keepdims=True)
    zc = z - mu
    var = jnp.mean(jnp.square(zc), axis=-1, keepdims=True)
    return zc * lax.rsqrt(var + LN_EPS) * g + b


def _out_kernel(x_ref, up_ref, uc_ref, un_ref, o_ref, wg_ref, bg_ref, dw_ref, dwb_ref, cg_ref, cb_ref,
                wc_ref, bc_ref, wa_ref, wo_ref, bo_ref, g_ref, b_ref, y_ref, ubuf, ushift, conv_sc, gate_sc, merge_sc,
                *, tiles_per_seq, alpha):
    tm = x_ref.shape[0]
    i = pl.program_id(0)
    pos = i % tiles_per_seq
    ubuf[0:HALO_ROWS, :] = jnp.where(pos > 0, up_ref[...], 0.0)
    ubuf[HALO_ROWS:HALO_ROWS + tm, :] = uc_ref[...]
    ubuf[HALO_ROWS + tm:2 * HALO_ROWS + tm, :] = jnp.where(pos < tiles_per_seq - 1, un_ref[...], 0.0)

    W = CONV_WIDTH
    xb = x_ref[...].astype(_BF16)
    gate_sc[...] = _dot(xb, wg_ref[...]) + bg_ref[...]
    oa = o_ref[...] * jax.nn.silu(gate_sc[:, W:2 * W])
    merge_sc[...] = jax.nn.sigmoid(gate_sc[:, 2 * W + D_MODEL:]) * _dot(oa.astype(_BF16), wa_ref[...])

    span = ushift.shape[1]
    for b in range(1, SUBLANES):
        ushift[b - 1, :, :] = ubuf[b:b + span, :]

    rc = CONV_ROW_CHUNK
    base = HALO_ROWS - CONV_PAD
    for r in range(0, tm, rc):
        acc = jnp.broadcast_to(dwb_ref[...], (rc, CONV_WIDTH))
        for t in range(CONV_KERNEL):
            a, b = divmod(base + t, SUBLANES)
            lo = r + a * SUBLANES
            window = ubuf[lo:lo + rc, :] if b == 0 else ushift[b - 1, lo:lo + rc, :]
            acc = acc + window * dw_ref[t:t + 1, :]
        conv_sc[r:r + rc, :] = acc

    uc = jax.nn.silu(_layer_norm(conv_sc[...], cg_ref[...], cb_ref[...])) * jax.nn.silu(gate_sc[:, 0:W])
    y_conv = _dot(uc.astype(_BF16), wc_ref[...]) + bc_ref[...]
    merged = jax.nn.sigmoid(gate_sc[:, 2 * W:2 * W + D_MODEL]) * y_conv + merge_sc[...]
    out = _dot(merged.astype(_BF16), wo_ref[...]) + bo_ref[...]
    y_ref[...] = _layer_norm(alpha * x_ref[...] + out, g_ref[...], b_ref[...])


def _out_call(x, u, o, w_g, b_g, dw, dwb, cg, cb, wc, bc, wa, wo, bo, g, b, seq_len, alpha):
    n_tok = x.shape[0]
    tm = OUT_ROWS
    tiles_per_seq = seq_len // tm
    hb = tm // HALO_ROWS
    n_halo = n_tok // HALO_ROWS
    row = lambda i: (i, 0)
    const = lambda i: (0, 0)
    prev = lambda i: (jnp.maximum(i * hb - 1, 0), 0)
    nxt = lambda i: (jnp.minimum((i + 1) * hb, n_halo - 1), 0)

    def resident(arr):
        return pl.BlockSpec(arr.shape, const, pipeline_mode=pl.Buffered(1))

    kernel = functools.partial(_out_kernel, tiles_per_seq=tiles_per_seq, alpha=alpha)
    return pl.pallas_call(
        kernel,
        grid=(n_tok // tm,),
        in_specs=[
            pl.BlockSpec((tm, D_MODEL), row),
            pl.BlockSpec((HALO_ROWS, CONV_WIDTH), prev),
            pl.BlockSpec((tm, CONV_WIDTH), row),
            pl.BlockSpec((HALO_ROWS, CONV_WIDTH), nxt),
            pl.BlockSpec((tm, ATTN_WIDTH), row),
            resident(w_g), resident(b_g), resident(dw), resident(dwb), resident(cg), resident(cb),
            resident(wc), resident(bc), resident(wa), resident(wo), resident(bo), resident(g), resident(b),
        ],
        out_specs=pl.BlockSpec((tm, D_MODEL), row),
        out_shape=jax.ShapeDtypeStruct((n_tok, D_MODEL), _F32),
        scratch_shapes=[
            pltpu.VMEM((tm + 2 * HALO_ROWS, CONV_WIDTH), _F32),
            pltpu.VMEM((SUBLANES - 1, tm + 2 * HALO_ROWS - SUBLANES, CONV_WIDTH), _F32),
            pltpu.VMEM((tm, CONV_WIDTH), _F32),
            pltpu.VMEM((tm, w_g.shape[1]), _F32),
            pltpu.VMEM((tm, D_MODEL), _F32),
        ],
        compiler_params=pltpu.CompilerParams(
            dimension_semantics=("arbitrary",), vmem_limit_bytes=VMEM_LIMIT_BYTES),
        name="out",
    )(x, u, u, u, o, w_g, b_g, dw, dwb, cg, cb, wc, bc, wa, wo, bo, g, b)


def _rope_tables(seq_len):
    half = HEAD_DIM // 2
    pos = jnp.arange(seq_len, dtype=_F32)
    inv_freq = 1.0 / jnp.power(ROPE_THETA, jnp.arange(0, HEAD_DIM, 2, dtype=_F32) / HEAD_DIM)
    ang = pos[:, None] * jnp.tile(inv_freq, LANES // half)[None, :]
    first_half = (jnp.arange(LANES) % HEAD_DIM < half)[None, :]
    sin = jnp.sin(ang)
    return jnp.cos(ang), jnp.where(first_half, -sin, 0.0), jnp.where(first_half, 0.0, sin)


def _encoder_layer(x, seq_len, rope, p, lam_init, alpha):
    cos, sin_lo, sin_hi = rope
    u, q, k, v = _proj_call(x, p["w_a"], p["b_a"], cos, sin_lo, sin_hi, seq_len)
    o = _attn_call(q, k, v, p["lq1"], p["lk1"], p["lq2"], p["lk2"], p["subln_g"], seq_len, lam_init)
    return _out_call(x, u, o, p["w_g"], p["b_g"], p["dw"], p["dwb"], p["cg"], p["cb"], p["wc"], p["bc"],
                     p["wa"], p["wo"], p["bo"], p["g"], p["b"], seq_len, alpha)


def _layer_params(l, w_in, b_in, conv_dw, conv_dw_b, conv_ln_g, conv_ln_b, w_conv_proj, b_conv_proj,
                  lam_q1, lam_k1, lam_q2, lam_k2, subln_g, w_attn_o, w_out, b_out, ln_g, ln_b):
    W = CONV_WIDTH
    row = lambda a: a[l][None, :]
    w, b = w_in[l], b_in[l]
    a_cols = [slice(0, 2 * W), slice(3 * W, 6 * W)]
    g_cols = [slice(2 * W, 3 * W), slice(6 * W, w.shape[1])]
    return {
        "w_a": jnp.concatenate([w[:, s] for s in a_cols], axis=1).astype(_BF16),
        "b_a": jnp.concatenate([b[s] for s in a_cols])[None, :],
        "w_g": jnp.concatenate([w[:, s] for s in g_cols], axis=1).astype(_BF16),
        "b_g": jnp.concatenate([b[s] for s in g_cols])[None, :],
        "dw": conv_dw[l], "dwb": row(conv_dw_b), "cg": row(conv_ln_g), "cb": row(conv_ln_b),
        "wc": w_conv_proj[l].astype(_BF16), "bc": row(b_conv_proj),
        "lq1": row(lam_q1), "lk1": row(lam_k1), "lq2": row(lam_q2), "lk2": row(lam_k2),
        "subln_g": row(subln_g), "wa": w_attn_o[l].astype(_BF16),
        "wo": w_out[l].astype(_BF16), "bo": row(b_out), "g": row(ln_g), "b": row(ln_b),
    }


def kernel(x_prompt, x_sample, w_in, b_in, conv_dw, conv_dw_b, conv_ln_g, conv_ln_b, w_conv_proj, b_conv_proj,
           lam_q1, lam_k1, lam_q2, lam_k2, subln_g, w_attn_o, w_out, b_out, ln_g, ln_b):
    depth = w_in.shape[0]
    alpha = (2.0 * depth) ** 0.25
    weights = (w_in, b_in, conv_dw, conv_dw_b, conv_ln_g, conv_ln_b, w_conv_proj, b_conv_proj,
               lam_q1, lam_k1, lam_q2, lam_k2, subln_g, w_attn_o, w_out, b_out, ln_g, ln_b)
    rope = _rope_tables(max(x_prompt.shape[1], x_sample.shape[1]))
    params = [_layer_params(l, *weights) for l in range(depth)]
    ys = []
    for x in (x_prompt, x_sample):
        n_seq, seq_len, d = x.shape
        y = x.reshape(n_seq * seq_len, d)
        for l in range(depth):
            lam_init = 0.8 - 0.6 * math.exp(-0.3 * l)
            y = _encoder_layer(y, seq_len, rope, params[l], lam_init, alpha)
        ys.append(y.reshape(n_seq, seq_len, d))
    return tuple(ys)
```
